```python
import jax, jax.numpy as jnp
from jax import lax
import numpy as np

D_MODEL = 4096
BATCH = 8
SEQ = 2048
DEPTH = 2

HEAD_DIM = 128
N_HEADS_A = 16
N_HEADS_B = 16
DILATED_PATTERNS = ((128, 1), (512, 4), (2048, 16))
MOBA_BLOCK = 256
MOBA_TOPK = 3
MOBA_QCHUNK = 16
N_HEADS_C = 16
SB_QBLOCK = 128
N_HEADS_D = 8
DQK_D = 128
DV_D = 256
MLSTM_CHUNK = 64
CONV_D = 4
D_FF = 11008
FFN_CONV = 3
LN_EPS = 1e-5
ALPHA = (2 * DEPTH) ** 0.25
BETA = (8 * DEPTH) ** -0.25
N_EVEN = (DEPTH + 1) // 2
N_ODD = DEPTH // 2

W_A = N_HEADS_A * HEAD_DIM
W_B = N_HEADS_B * HEAD_DIM
W_C = N_HEADS_C * HEAD_DIM
QK_D = N_HEADS_D * DQK_D
W_D = N_HEADS_D * DV_D
IN_AB = 3 * (W_A + W_B)
IN_CD = 3 * W_C + 2 * QK_D + 2 * W_D + 2 * N_HEADS_D
MIX_AB = W_A + W_B
MIX_CD = W_C + W_D

kernel_name = "hybrid_dilated_moba_stickbreak_mlstm"


def _split(t, sizes):
    return jnp.split(t, [int(s) for s in np.cumsum(sizes)[:-1]], axis=-1)


def layer_norm(x, g, b):
    xf = x.astype(jnp.float32)
    mu = xf.mean(-1, keepdims=True)
    var = jnp.mean(jnp.square(xf - mu), -1, keepdims=True)
    return ((xf - mu) * lax.rsqrt(var + LN_EPS)).astype(x.dtype) * g + b


def causal_dwconv(x, w, b):
    K, S = w.shape[0], x.shape[1]
    xp = jnp.pad(x, ((0, 0), (K - 1, 0), (0, 0)))
    return sum(w[k] * xp[:, k:k + S] for k in range(K)) + b


def dilated_branch(q, k, v, window, dilation):
    Bsz, S, H, Dh = q.shape
    L = S // dilation
    nw = window // dilation
    blk = nw
    nb = -(-L // blk)
    Lp = nb * blk

    def strided(t):
        t = t.reshape(Bsz, L, dilation, H, Dh).transpose(0, 2, 3, 1, 4)
        return jnp.pad(t, ((0, 0), (0, 0), (0, 0), (0, Lp - L), (0, 0)))

    def band(t):
        tp = jnp.pad(t, ((0, 0), (0, 0), (0, 0), (blk, 0), (0, 0)))
        tp = tp.reshape(Bsz, dilation, H, nb + 1, blk, Dh)
        return jnp.concatenate([tp[:, :, :, :-1], tp[:, :, :, 1:]], axis=4)

    qb = strided(q).reshape(Bsz, dilation, H, nb, blk, Dh)
    kb, vb = band(strided(k)), band(strided(v))
    s = jnp.einsum('bdhnqe,bdhnke->bdhnqk', qb, kb).astype(jnp.float32) * (Dh ** -0.5)
    qi = jnp.arange(blk)[:, None]
    ki = jnp.arange(2 * blk)[None, :] - blk
    dist = qi - ki
    kpos = jnp.arange(nb)[:, None, None] * blk + ki[None]
    valid = (dist >= 0) & (dist <= nw) & (kpos >= 0)
    s = jnp.where(valid, s, -jnp.inf)
    mx = s.max(-1)
    p = jnp.exp(s - mx[..., None])
    den = p.sum(-1)
    num = jnp.einsum('bdhnqk,bdhnke->bdhnqe', p.astype(vb.dtype), vb)

    def unstrided(t):
        rest = t.shape[5:]
        t = t.reshape(Bsz, dilation, H, Lp, *rest)[:, :, :, :L]
        t = jnp.moveaxis(t, 3, 1)
        return t.reshape(Bsz, S, H, *rest)

    return unstrided(num), unstrided(den), unstrided(mx)


def dilated_attention(q, k, v):
    outs = [dilated_branch(q, k, v, w, d) for (w, d) in DILATED_PATTERNS]
    m = jnp.max(jnp.stack([o[2] for o in outs]), axis=0)
    num = sum(o[0].astype(jnp.float32) * jnp.exp(o[2] - m)[..., None] for o in outs)
    den = sum(o[1] * jnp.exp(o[2] - m) for o in outs)
    return (num / den[..., None]).astype(q.dtype)


def moba_attention(q, k, v):
    Bsz, S, H, Dh = q.shape
    nblk = -(-S // MOBA_BLOCK)
    Sp = nblk * MOBA_BLOCK
    topk = min(MOBA_TOPK, nblk)
    pad = lambda t: jnp.pad(t, ((0, 0), (0, Sp - S), (0, 0), (0, 0))).transpose(0, 2, 1, 3)
    qh = pad(q)
    kh = pad(k).reshape(Bsz, H, nblk, MOBA_BLOCK, Dh)
    vh = pad(v).reshape(Bsz, H, nblk, MOBA_BLOCK, Dh)
    kmean = kh.mean(axis=3)
    bidx = jnp.arange(Bsz)[:, None, None, None]
    hidx = jnp.arange(H)[None, :, None, None]
    scale = Dh ** -0.5

    def chunk(ci):
        t0 = ci * MOBA_QCHUNK
        qc = lax.dynamic_slice_in_dim(qh, t0, MOBA_QCHUNK, axis=2)
        own = t0 // MOBA_BLOCK
        gate = jnp.einsum('bhqe,bhne->bhqn', qc, kmean).astype(jnp.float32)
        gate = jnp.where(jnp.arange(nblk) < own, gate, -jnp.inf)
        _, idx = lax.top_k(gate, topk)
        sel_ok = idx < own
        ksel = kh[bidx, hidx, idx]
        vsel = vh[bidx, hidx, idx].reshape(Bsz, H, MOBA_QCHUNK, topk * MOBA_BLOCK, Dh)
        s_sel = jnp.einsum('bhqe,bhqjke->bhqjk', qc, ksel).reshape(Bsz, H, MOBA_QCHUNK, topk * MOBA_BLOCK)
        s_sel = jnp.where(jnp.repeat(sel_ok, MOBA_BLOCK, axis=-1), s_sel.astype(jnp.float32), -jnp.inf)
        ko = lax.dynamic_index_in_dim(kh, own, axis=2, keepdims=False)
        vo = lax.dynamic_index_in_dim(vh, own, axis=2, keepdims=False)
        s_own = jnp.einsum('bhqe,bhke->bhqk', qc, ko).astype(jnp.float32)
        qpos = t0 + jnp.arange(MOBA_QCHUNK)
        kpos = own * MOBA_BLOCK + jnp.arange(MOBA_BLOCK)
        s_own = jnp.where(kpos[None, :] <= qpos[:, None], s_own, -jnp.inf)
        p = jax.nn.softmax(jnp.concatenate([s_sel, s_own], axis=-1) * scale, axis=-1)
        p_sel, p_own = p[..., :topk * MOBA_BLOCK], p[..., topk * MOBA_BLOCK:]
        return (jnp.einsum('bhqk,bhqke->bhqe', p_sel.astype(v.dtype), vsel)
                + jnp.einsum('bhqk,bhke->bhqe', p_own.astype(v.dtype), vo))

    out = lax.map(chunk, jnp.arange(Sp // MOBA_QCHUNK))
    out = out.transpose(1, 0, 3, 2, 4).reshape(Bsz, Sp, H, Dh)[:, :S]
    return out.astype(q.dtype)


def stick_breaking_attention(q, k, v):
    Bsz, S, H, Dh = q.shape
    qh, kh, vh = (t.transpose(0, 2, 1, 3) for t in (q, k, v))
    kpos = jnp.arange(S)

    def block(bi):
        t0 = bi * SB_QBLOCK
        qc = lax.dynamic_slice_in_dim(qh, t0, SB_QBLOCK, axis=2)
        z = jnp.einsum('bhqe,bhke->bhqk', qc, kh).astype(jnp.float32) * (Dh ** -0.5)
        qpos = t0 + jnp.arange(SB_QBLOCK)
        causal = kpos[None, :] < qpos[:, None]
        sp = jnp.where(causal, jax.nn.softplus(z), 0.0)
        rc = lax.cumsum(sp, axis=3, reverse=True)
        A = jnp.exp(jnp.where(causal, z - rc, -jnp.inf))
        return jnp.einsum('bhqk,bhke->bhqe', A.astype(v.dtype), vh)

    out = lax.map(block, jnp.arange(S // SB_QBLOCK))
    return out.transpose(1, 0, 3, 2, 4).reshape(Bsz, S, H, Dh).astype(q.dtype)


def mlstm_chunkwise(q, k, v, i_pre, f_pre):
    f32 = jnp.float32
    Bsz, S, H, Dk = q.shape
    Dv = v.shape[-1]
    L = MLSTM_CHUNK
    nc = S // L
    q, v = q.astype(f32), v.astype(f32)
    k = k.astype(f32) * (Dk ** -0.5)
    log_f = jax.nn.log_sigmoid(f_pre.astype(f32))
    log_i = i_pre.astype(f32)

    def to_chunks(t):
        t = t.reshape(Bsz, nc, L, H, *t.shape[3:])
        return jnp.moveaxis(t, [1, 3], [0, 2])

    tri = jnp.tril(jnp.ones((L, L), dtype=bool))

    def step(carry, xs):
        C, n, m = carry
        qc, kc, vc, lic, lfc = xs
        b = jnp.cumsum(lfc, axis=-1)
        g = b[..., -1]
        Dm = jnp.where(tri, b[..., :, None] - b[..., None, :] + lic[..., None, :], -jnp.inf)
        inter = b + m[..., None]
        m_t = jnp.maximum(inter, Dm.max(-1))
        P = jnp.exp(Dm - m_t[..., None])
        w_inter = jnp.exp(inter - m_t)
        Sqk = jnp.einsum('bhte,bhse->bhts', qc, kc) * P
        num = (w_inter[..., None] * jnp.einsum('bhve,bhte->bhtv', C, qc)
               + jnp.einsum('bhts,bhsv->bhtv', Sqk, vc))
        den = w_inter * jnp.einsum('bhe,bhte->bht', n, qc) + Sqk.sum(-1)
        h = num / jnp.maximum(jnp.abs(den), jnp.exp(-m_t))[..., None]
        dec = g[..., None] - b + lic
        m_new = jnp.maximum(g + m, dec.max(-1))
        wk = jnp.exp(dec - m_new[..., None])
        sc = jnp.exp(g + m - m_new)
        C_new = sc[..., None, None] * C + jnp.einsum('bhs,bhsv,bhse->bhve', wk, vc, kc)
        n_new = sc[..., None] * n + jnp.einsum('bhs,bhse->bhe', wk, kc)
        return (C_new, n_new, m_new), h

    init = (jnp.zeros((Bsz, H, Dv, Dk), f32), jnp.zeros((Bsz, H, Dk), f32), jnp.zeros((Bsz, H), f32))
    xs = (to_chunks(q), to_chunks(k), to_chunks(v), to_chunks(log_i), to_chunks(log_f))
    _, h = lax.scan(step, init, xs)
    return jnp.moveaxis(h, [0, 2], [1, 3]).reshape(Bsz, S, H, Dv)


def mixer_ab(x, w_in, w_out):
    Bsz, S, _ = x.shape
    qa, ka, va, qb, kb, vb = _split(x @ w_in, [W_A, W_A, W_A, W_B, W_B, W_B])
    heads = lambda t: t.reshape(Bsz, S, -1, HEAD_DIM)
    ya = dilated_attention(heads(qa), heads(ka), heads(va)).reshape(Bsz, S, W_A)
    yb = moba_attention(heads(qb), heads(kb), heads(vb)).reshape(Bsz, S, W_B)
    return jnp.concatenate([ya, yb], axis=-1).astype(x.dtype) @ w_out


def mixer_cd(x, w_in, b_if, conv_w, conv_b, norm_g, w_out):
    Bsz, S, _ = x.shape
    qc, kc, vc, qkd, vd, od, gd = _split(x @ w_in, [W_C, W_C, W_C, 2 * QK_D, W_D, W_D, 2 * N_HEADS_D])
    hc = lambda t: t.reshape(Bsz, S, N_HEADS_C, HEAD_DIM)
    yc = stick_breaking_attention(hc(qc), hc(kc), hc(vc)).reshape(Bsz, S, W_C)
    qkd = jax.nn.silu(causal_dwconv(qkd, conv_w, conv_b))
    qd, kd = _split(qkd, [QK_D, QK_D])
    gates = gd + b_if
    h = mlstm_chunkwise(qd.reshape(Bsz, S, N_HEADS_D, DQK_D), kd.reshape(Bsz, S, N_HEADS_D, DQK_D),
                        vd.reshape(Bsz, S, N_HEADS_D, DV_D), gates[..., :N_HEADS_D], gates[..., N_HEADS_D:])
    mu = h.mean(-1, keepdims=True)
    var = jnp.mean(jnp.square(h - mu), -1, keepdims=True)
    h = ((h - mu) * lax.rsqrt(var + LN_EPS)).reshape(Bsz, S, W_D) * norm_g
    yd = jax.nn.sigmoid(od.astype(jnp.float32)) * h
    return jnp.concatenate([yc, yd.astype(x.dtype)], axis=-1).astype(x.dtype) @ w_out


def conv_ffn(x, w_up, w_gate, conv_w, conv_b, w_down):
    u = x @ w_up
    g = causal_dwconv(x @ w_gate, conv_w, conv_b)
    return (jax.nn.silu(g) * u) @ w_down


def setup_inputs(seed: int = 0) -> dict:
    key = jax.random.key(seed)
    ks = jax.random.split(key, 18)
    nrm = lambda k, shape, scale: jax.random.normal(k, shape, jnp.float32) * scale
    NE, NO = N_EVEN, N_ODD
    x = nrm(ks[0], (BATCH, SEQ, D_MODEL), 1.0)
    w_in_ab = nrm(ks[1], (NE, D_MODEL, IN_AB), D_MODEL ** -0.5)
    w_out_ab = nrm(ks[2], (NE, MIX_AB, D_MODEL), BETA * MIX_AB ** -0.5)
    w_in_cd = nrm(ks[3], (NO, D_MODEL, IN_CD), D_MODEL ** -0.5)
    b_if_cd = jnp.concatenate([
        nrm(ks[4], (NO, N_HEADS_D), 0.1),
        jnp.broadcast_to(jnp.linspace(3.0, 6.0, N_HEADS_D), (NO, N_HEADS_D)) + nrm(ks[5], (NO, N_HEADS_D), 0.1)], axis=-1)
    conv_cd = nrm(ks[6], (NO, CONV_D, 2 * QK_D), CONV_D ** -0.5)
    conv_cd_b = nrm(ks[7], (NO, 2 * QK_D), 0.02)
    norm_cd_g = 1.0 + nrm(ks[8], (NO, W_D), 0.02)
    w_out_cd = nrm(ks[9], (NO, MIX_CD, D_MODEL), BETA * MIX_CD ** -0.5)
    ffn_w_up = nrm(ks[10], (DEPTH, D_MODEL, D_FF), D_MODEL ** -0.5)
    ffn_w_gate = nrm(ks[11], (DEPTH, D_MODEL, D_FF), D_MODEL ** -0.5)
    ffn_conv = nrm(ks[12], (DEPTH, FFN_CONV, D_FF), FFN_CONV ** -0.5)
    ffn_conv_b = nrm(ks[13], (DEPTH, D_FF), 0.02)
    ffn_w_down = nrm(ks[14], (DEPTH, D_FF, D_MODEL), BETA * D_FF ** -0.5)
    ln_g = 1.0 + nrm(ks[15], (DEPTH, 2, D_MODEL), 0.02)
    ln_b = nrm(ks[16], (DEPTH, 2, D_MODEL), 0.02)
    return {"x": x, "w_in_ab": w_in_ab, "w_out_ab": w_out_ab, "w_in_cd": w_in_cd,
            "b_if_cd": b_if_cd, "conv_cd": conv_cd, "conv_cd_b": conv_cd_b, "norm_cd_g": norm_cd_g,
            "w_out_cd": w_out_cd, "ffn_w_up": ffn_w_up, "ffn_w_gate": ffn_w_gate, "ffn_conv": ffn_conv,
            "ffn_conv_b": ffn_conv_b, "ffn_w_down": ffn_w_down, "ln_g": ln_g, "ln_b": ln_b}


def reference(x, w_in_ab, w_out_ab, w_in_cd, b_if_cd, conv_cd, conv_cd_b, norm_cd_g, w_out_cd,
              ffn_w_up, ffn_w_gate, ffn_conv, ffn_conv_b, ffn_w_down, ln_g, ln_b):
    for l in range(DEPTH):
        j = l // 2
        if l % 2 == 0:
            y = mixer_ab(x, w_in_ab[j], w_out_ab[j])
        else:
            y = mixer_cd(x, w_in_cd[j], b_if_cd[j], conv_cd[j], conv_cd_b[j], norm_cd_g[j], w_out_cd[j])
        x = layer_norm(ALPHA * x + y, ln_g[l, 0], ln_b[l, 0])
        y = conv_ffn(x, ffn_w_up[l], ffn_w_gate[l], ffn_conv[l], ffn_conv_b[l], ffn_w_down[l])
        x = layer_norm(ALPHA * x + y, ln_g[l, 1], ln_b[l, 1])
    return x
```

```python
import functools

import numpy as np
import jax
import jax.numpy as jnp
from jax import lax
from jax.experimental import pallas as pl
from jax.experimental.pallas import tpu as pltpu

F32 = jnp.float32
BF16 = jnp.bfloat16

D_MODEL = 4096
DEPTH = 2
HEAD_DIM = 128
N_HEADS_A = 16
N_HEADS_B = 16
N_HEADS_C = 16
N_HEADS_D = 8
DQK_D = 128
DV_D = 256
D_FF = 11008
LN_EPS = 1e-5
ALPHA = (2 * DEPTH) ** 0.25
MOBA_BLOCK = 256
MOBA_TOPK = 3
DILATED_PATTERNS = ((128, 1), (512, 4), (2048, 16))

W_A = N_HEADS_A * HEAD_DIM
W_B = N_HEADS_B * HEAD_DIM
W_C = N_HEADS_C * HEAD_DIM
QK_D = N_HEADS_D * DQK_D
W_D = N_HEADS_D * DV_D
IN_MAIN = 12288
GATE_PAD = 128

LANE = 128
ATT_BLK = 256
MLSTM_L = 256
D_FF_PAD = 11264
FFN_TN = 512
NEG = -1e30

VMEM_LIMIT = 56 * 1024 * 1024


def _cparams(sem):
    return pltpu.CompilerParams(dimension_semantics=sem, vmem_limit_bytes=VMEM_LIMIT)


def _dot(a, b):
    return jnp.dot(a, b, preferred_element_type=F32)


def _dot_nt(a, b):
    return lax.dot_general(a, b, (((1,), (1,)), ((), ())), preferred_element_type=F32)


def _dot_tn(a, b):
    return lax.dot_general(a, b, (((0,), (0,)), ((), ())), preferred_element_type=F32)


def _split3(x):
    h = x.astype(BF16)
    r = x - h.astype(F32)
    m = r.astype(BF16)
    l = (r - m.astype(F32)).astype(BF16)
    return h, m, l


def _mm_kernel(a_ref, w_ref, o_ref):
    o_ref[...] = _dot(a_ref[...], w_ref[...]).astype(o_ref.dtype)


def _matmul(a, w, out_dtype, tm=1024, tn=1024):
    M, K = a.shape
    N = w.shape[1]
    tn = min(tn, N)
    return pl.pallas_call(
        _mm_kernel,
        out_shape=jax.ShapeDtypeStruct((M, N), out_dtype),
        grid=(M // tm, N // tn),
        in_specs=[pl.BlockSpec((tm, K), lambda i, j: (i, 0)),
                  pl.BlockSpec((K, tn), lambda i, j: (0, j))],
        out_specs=pl.BlockSpec((tm, tn), lambda i, j: (i, j)),
        compiler_params=_cparams(("parallel", "arbitrary")),
        name="matmul",
    )(a, w)


def _mm2_resid_kernel(a1_ref, a2_ref, w1_ref, w2_ref, x_ref, o_ref):
    y = _dot(a1_ref[...], w1_ref[...]) + _dot(a2_ref[...], w2_ref[...])
    o_ref[...] = ALPHA * x_ref[...] + y


def _out_proj_resid(a1, a2, w, x, tm=1024, tn=512):
    M, K1 = a1.shape
    K2 = a2.shape[1]
    assert K1 == K2
    N = w.shape[1]
    return pl.pallas_call(
        _mm2_resid_kernel,
        out_shape=jax.ShapeDtypeStruct((M, N), F32),
        grid=(M // tm, N // tn),
        in_specs=[pl.BlockSpec((tm, K1), lambda i, j: (i, 0)),
                  pl.BlockSpec((tm, K2), lambda i, j: (i, 0)),
                  pl.BlockSpec((K1, tn), lambda i, j: (0, j)),
                  pl.BlockSpec((K2, tn), lambda i, j: (1, j)),
                  pl.BlockSpec((tm, tn), lambda i, j: (i, j))],
        out_specs=pl.BlockSpec((tm, tn), lambda i, j: (i, j)),
        compiler_params=_cparams(("parallel", "arbitrary")),
        name="out_proj_resid",
    )(a1, a2, w, w, x)


def _mmk_resid_kernel(a_ref, w_ref, x_ref, o_ref, acc_ref):
    k = pl.program_id(2)

    @pl.when(k == 0)
    def _():
        acc_ref[...] = _dot(a_ref[...], w_ref[...])

    @pl.when(k > 0)
    def _():
        acc_ref[...] += _dot(a_ref[...], w_ref[...])

    @pl.when(k == pl.num_programs(2) - 1)
    def _():
        o_ref[...] = ALPHA * x_ref[...] + acc_ref[...]


def _ffn_down_resid(h, w, x, tm=1024, tn=1024, tk=2816):
    M, K = h.shape
    N = w.shape[1]
    return pl.pallas_call(
        _mmk_resid_kernel,
        out_shape=jax.ShapeDtypeStruct((M, N), F32),
        grid=(M // tm, N // tn, K // tk),
        in_specs=[pl.BlockSpec((tm, tk), lambda i, j, k: (i, k)),
                  pl.BlockSpec((tk, tn), lambda i, j, k: (k, j)),
                  pl.BlockSpec((tm, tn), lambda i, j, k: (i, j))],
        out_specs=pl.BlockSpec((tm, tn), lambda i, j, k: (i, j)),
        scratch_shapes=[pltpu.VMEM((tm, tn), F32)],
        compiler_params=_cparams(("parallel", "arbitrary", "arbitrary")),
        name="ffn_down_resid",
    )(h, w, x)


def _ln_kernel(z_ref, g_ref, b_ref, o_ref, ob_ref):
    z = z_ref[...]
    mu = jnp.mean(z, axis=-1, keepdims=True)
    zc = z - mu
    var = jnp.mean(zc * zc, axis=-1, keepdims=True)
    y = zc * lax.rsqrt(var + LN_EPS) * g_ref[...] + b_ref[...]
    o_ref[...] = y
    ob_ref[...] = y.astype(BF16)


def _layer_norm(z, g, b, tr=256):
    M, N = z.shape
    return pl.pallas_call(
        _ln_kernel,
        out_shape=(jax.ShapeDtypeStruct((M, N), F32), jax.ShapeDtypeStruct((M, N), BF16)),
        grid=(M // tr,),
        in_specs=[pl.BlockSpec((tr, N), lambda i: (i, 0)),
                  pl.BlockSpec((1, N), lambda i: (0, 0)),
                  pl.BlockSpec((1, N), lambda i: (0, 0))],
        out_specs=(pl.BlockSpec((tr, N), lambda i: (i, 0)),
                   pl.BlockSpec((tr, N), lambda i: (i, 0))),
        compiler_params=_cparams(("parallel",)),
        name="layer_norm",
    )(z, g.reshape(1, N), b.reshape(1, N))


def _ffn_up_kernel(x_ref, wu_ref, wg_ref, cw_ref, cb_ref, h_ref, tail_ref, *, tiles_per_seq):
    i = pl.program_id(1)
    x = x_ref[...]
    u = _dot(x, wu_ref[...])
    g = _dot(x, wg_ref[...])
    tm = g.shape[0]

    @pl.when(i % tiles_per_seq == 0)
    def _():
        tail_ref[...] = jnp.zeros_like(tail_ref)

    tail = tail_ref[...]
    gm1 = tail[7:8, :]
    gm2 = tail[6:7, :]
    row = lax.broadcasted_iota(jnp.int32, g.shape, 0)
    g1 = jnp.where(row == 0, gm1, pltpu.roll(g, 1, 0))
    g2 = jnp.where(row == 0, gm2, jnp.where(row == 1, gm1, pltpu.roll(g, 2, 0)))
    tail_ref[...] = g[tm - 8:, :]
    cw = cw_ref[...]
    gc = cw[0:1, :] * g2 + cw[1:2, :] * g1 + cw[2:3, :] * g + cb_ref[...]
    h_ref[...] = (gc * jax.nn.sigmoid(gc) * u).astype(h_ref.dtype)


def _ffn_up(xb, wu, wg, cw, cb, seq, tm=1024, tn=FFN_TN):
    M, K = xb.shape
    N = wu.shape[1]
    kern = functools.partial(_ffn_up_kernel, tiles_per_seq=seq // tm)
    return pl.pallas_call(
        kern,
        out_shape=jax.ShapeDtypeStruct((M, N), BF16),
        grid=(N // tn, M // tm),
        in_specs=[pl.BlockSpec((tm, K), lambda j, i: (i, 0)),
                  pl.BlockSpec((K, tn), lambda j, i: (0, j)),
                  pl.BlockSpec((K, tn), lambda j, i: (0, j)),
                  pl.BlockSpec((3, tn), lambda j, i: (0, j)),
                  pl.BlockSpec((1, tn), lambda j, i: (0, j))],
        out_specs=pl.BlockSpec((tm, tn), lambda j, i: (i, j)),
        scratch_shapes=[pltpu.VMEM((8, tn), F32)],
        compiler_params=_cparams(("arbitrary", "arbitrary")),
        name="ffn_up",
    )(xb, wu, wg, cw, cb)


def _dilated_bias_table():
    r = np.arange(ATT_BLK)[:, None]
    c = np.arange(ATT_BLK)[None, :]
    tables = []
    for dl in range(4):
        dist = dl * ATT_BLK + r - c
        cnt = np.zeros_like(dist)
        for (w, d) in DILATED_PATTERNS:
            cnt += ((dist >= 0) & (dist % d == 0) & (dist <= w)).astype(dist.dtype)
        with np.errstate(divide="ignore"):
            tables.append(np.where(cnt > 0, np.log(np.maximum(cnt, 1)), NEG))
    return np.stack(tables).astype(np.float32)


def _attn_a_kernel(q_ref, k_ref, v_ref, bias_ref, o_ref, *, scale):
    S = q_ref.shape[0]
    T = ATT_BLK
    for i in range(S // T):
        q = q_ref[i * T:(i + 1) * T, :]

        def step(j, bias, carry):
            m, l, acc = carry
            off = j * T if isinstance(j, int) else pl.multiple_of(j * T, T)
            kb = k_ref[pl.ds(off, T), :]
            vb = v_ref[pl.ds(off, T), :]
            s = _dot_nt(q, kb) * scale + bias
            m_new = jnp.maximum(m, jnp.max(s, axis=-1, keepdims=True))
            a = jnp.exp(m - m_new)
            p = jnp.exp(s - m_new)
            l = a * l + jnp.sum(p, axis=-1, keepdims=True)
            acc = a * acc + _dot(p.astype(BF16), vb)
            return m_new, l, acc

        carry = (jnp.full((T, 1), NEG, F32), jnp.zeros((T, 1), F32), jnp.zeros((T, HEAD_DIM), F32))
        for dl in range(min(3, i + 1)):
            carry = step(i - dl, bias_ref[dl], carry)
        if i >= 3:
            far = bias_ref[3]
            carry = lax.fori_loop(0, i - 2, lambda j, c: step(j, far, c), carry)
        m, l, acc = carry
        o_ref[i * T:(i + 1) * T, :] = (acc / l).astype(o_ref.dtype)


def _attn_a(qkv, bias, batch, seq):
    kern = functools.partial(_attn_a_kernel, scale=HEAD_DIM ** -0.5)
    H = N_HEADS_A
    return pl.pallas_call(
        kern,
        out_shape=jax.ShapeDtypeStruct((batch * seq, W_A), BF16),
        grid=(batch, H),
        in_specs=[pl.BlockSpec((seq, HEAD_DIM), lambda b, h: (b, h)),
                  pl.BlockSpec((seq, HEAD_DIM), lambda b, h: (b, H + h)),
                  pl.BlockSpec((seq, HEAD_DIM), lambda b, h: (b, 2 * H + h)),
                  pl.BlockSpec((4, ATT_BLK, ATT_BLK), lambda b, h: (0, 0, 0))],
        out_specs=pl.BlockSpec((seq, HEAD_DIM), lambda b, h: (b, h)),
        compiler_params=_cparams(("parallel", "parallel")),
        name="attn_dilated",
    )(qkv, qkv, qkv, bias)


def _moba_mean_matrix(seq):
    m = np.zeros((LANE, seq), np.float32)
    for j in range(seq // MOBA_BLOCK):
        m[j, j * MOBA_BLOCK:(j + 1) * MOBA_BLOCK] = 1.0 / MOBA_BLOCK
    return m


def _attn_b_kernel(q_ref, k_ref, v_ref, mean_ref, o_ref, *, scale):
    S = q_ref.shape[0]
    T = ATT_BLK
    kmean = _dot(mean_ref[...], k_ref[...])
    km_h, km_m, km_l = _split3(kmean)
    lane = lax.broadcasted_iota(jnp.int32, (T, LANE), 1)
    row = lax.broadcasted_iota(jnp.int32, (T, T), 0)
    col = lax.broadcasted_iota(jnp.int32, (T, T), 1)

    for i in range(S // T):
        q = q_ref[i * T:(i + 1) * T, :]
        kb = k_ref[i * T:(i + 1) * T, :]
        vb = v_ref[i * T:(i + 1) * T, :]
        s = jnp.where(col <= row, _dot_nt(q, kb) * scale, NEG)
        m = jnp.max(s, axis=-1, keepdims=True)
        p = jnp.exp(s - m)
        l = jnp.sum(p, axis=-1, keepdims=True)
        acc = _dot(p.astype(BF16), vb)
        if i > 0:
            gate = _dot_nt(q, km_h) + _dot_nt(q, km_m) + _dot_nt(q, km_l)
            valid = lane < i
            if i > MOBA_TOPK:
                rank = jnp.zeros((T, LANE), jnp.int32)
                for jp in range(i):
                    gj = gate[:, jp:jp + 1]
                    beats = (gj > gate) | ((gj == gate) & (jp < lane))
                    rank = rank + beats.astype(jnp.int32)
                sel = valid & (rank < MOBA_TOPK)
            else:
                sel = valid
            selbias = jnp.where(sel, 0.0, NEG)

            def step(j, carry):
                m, l, acc = carry
                off = pl.multiple_of(j * T, T)
                kj = k_ref[pl.ds(off, T), :]
                vj = v_ref[pl.ds(off, T), :]
                bj = jnp.sum(jnp.where(lane == j, selbias, 0.0), axis=-1, keepdims=True)
                s = _dot_nt(q, kj) * scale + bj
                m_new = jnp.maximum(m, jnp.max(s, axis=-1, keepdims=True))
                a = jnp.exp(m - m_new)
                p = jnp.exp(s - m_new)
                l = a * l + jnp.sum(p, axis=-1, keepdims=True)
                acc = a * acc + _dot(p.astype(BF16), vj)
                return m_new, l, acc

            m, l, acc = lax.fori_loop(0, i, step, (m, l, acc))
        o_ref[i * T:(i + 1) * T, :] = (acc / l).astype(o_ref.dtype)


def _attn_b(qkv, mean_mat, batch, seq):
    kern = functools.partial(_attn_b_kernel, scale=HEAD_DIM ** -0.5)
    H = N_HEADS_B
    base = 3 * N_HEADS_A
    return pl.pallas_call(
        kern,
        out_shape=jax.ShapeDtypeStruct((batch * seq, W_B), BF16),
        grid=(batch, H),
        in_specs=[pl.BlockSpec((seq, HEAD_DIM), lambda b, h: (b, base + h)),
                  pl.BlockSpec((seq, HEAD_DIM), lambda b, h: (b, base + H + h)),
                  pl.BlockSpec((seq, HEAD_DIM), lambda b, h: (b, base + 2 * H + h)),
                  pl.BlockSpec((LANE, seq), lambda b, h: (0, 0))],
        out_specs=pl.BlockSpec((seq, HEAD_DIM), lambda b, h: (b, h)),
        compiler_params=_cparams(("parallel", "parallel")),
        name="attn_moba",
    )(qkv, qkv, qkv, mean_mat)


def _attn_c_kernel(q_ref, k_ref, v_ref, o_ref, *, scale):
    S = q_ref.shape[0]
    T = ATT_BLK
    row = lax.broadcasted_iota(jnp.int32, (T, T), 0)
    col = lax.broadcasted_iota(jnp.int32, (T, T), 1)
    tri = jnp.where(row >= col, 1.0, 0.0).astype(BF16)
    strict = col < row

    def block(q, kb, vb, c, acc, diag):
        z = _dot_nt(q, kb) * scale
        sp = jnp.maximum(z, 0.0) + jnp.log(1.0 + jnp.exp(-jnp.abs(z)))
        if diag:
            sp = jnp.where(strict, sp, 0.0)
        sp_h = sp.astype(BF16)
        sp_l = (sp - sp_h.astype(F32)).astype(BF16)
        rc_in = _dot(sp_h, tri) + _dot(sp_l, tri)
        a = jnp.exp(z - (rc_in + c))
        if diag:
            a = jnp.where(strict, a, 0.0)
        acc = acc + _dot(a.astype(BF16), vb)
        c = c + rc_in[:, 0:1]
        return c, acc

    for i in range(S // T):
        q = q_ref[i * T:(i + 1) * T, :]
        c = jnp.zeros((T, 1), F32)
        acc = jnp.zeros((T, HEAD_DIM), F32)
        c, acc = block(q, k_ref[i * T:(i + 1) * T, :], v_ref[i * T:(i + 1) * T, :], c, acc, True)
        if i > 0:
            def step(t, carry):
                off = pl.multiple_of((i - 1 - t) * T, T)
                return block(q, k_ref[pl.ds(off, T), :], v_ref[pl.ds(off, T), :], carry[0], carry[1], False)

            c, acc = lax.fori_loop(0, i, step, (c, acc))
        o_ref[i * T:(i + 1) * T, :] = acc.astype(o_ref.dtype)


def _attn_c(proj, batch, seq):
    kern = functools.partial(_attn_c_kernel, scale=HEAD_DIM ** -0.5)
    H = N_HEADS_C
    return pl.pallas_call(
        kern,
        out_shape=jax.ShapeDtypeStruct((batch * seq, W_C), BF16),
        grid=(batch, H),
        in_specs=[pl.BlockSpec((seq, HEAD_DIM), lambda b, h: (b, h)),
                  pl.BlockSpec((seq, HEAD_DIM), lambda b, h: (b, H + h)),
                  pl.BlockSpec((seq, HEAD_DIM), lambda b, h: (b, 2 * H + h))],
        out_specs=pl.BlockSpec((seq, HEAD_DIM), lambda b, h: (b, h)),
        compiler_params=_cparams(("parallel", "parallel")),
        name="attn_stickbreak",
    )(proj, proj, proj)


def _conv4_silu(x, cw, cb):
    row = lax.broadcasted_iota(jnp.int32, x.shape, 0)
    y = cw[3:4, :] * x + cb
    for d in (1, 2, 3):
        xs = jnp.where(row >= d, pltpu.roll(x, d, 0), 0.0)
        y = y + cw[3 - d:4 - d, :] * xs
    return y * jax.nn.sigmoid(y)


def _mlstm_kernel(q_ref, k_ref, v_ref, o_ref, g_ref, cwq_ref, cbq_ref, cwk_ref, cbk_ref,
                  bif_ref, ng_ref, y_ref, qs_ref, ks_ref):
    h = pl.program_id(1)
    S = q_ref.shape[0]
    L = MLSTM_L
    qs_ref[...] = _conv4_silu(q_ref[...].astype(F32), cwq_ref[...], cbq_ref[...]).astype(BF16)
    ks_ref[...] = (_conv4_silu(k_ref[...].astype(F32), cwk_ref[...], cbk_ref[...])
                   * (DQK_D ** -0.5)).astype(BF16)

    row = lax.broadcasted_iota(jnp.int32, (L, L), 0)
    col = lax.broadcasted_iota(jnp.int32, (L, L), 1)
    lower = row >= col
    tril = jnp.where(lower, 1.0, 0.0).astype(BF16)
    lane = lax.broadcasted_iota(jnp.int32, (L, LANE), 1)
    sub = lax.broadcasted_iota(jnp.int32, (LANE, L), 0)
    ng = ng_ref[...]

    def chunk(ci, carry):
        Ct, n, m = carry
        off = pl.multiple_of(ci * L, L)
        gates = g_ref[pl.ds(off, L), :] + bif_ref[...]
        lsig = jnp.minimum(gates, 0.0) - jnp.log(1.0 + jnp.exp(-jnp.abs(gates)))
        li_c = jnp.sum(jnp.where(lane == h, gates, 0.0), axis=-1, keepdims=True)
        lf_c = jnp.sum(jnp.where(lane == h + N_HEADS_D, lsig, 0.0), axis=-1, keepdims=True)
        gates_t = gates.T
        lsig_t = lsig.T
        li_r = jnp.sum(jnp.where(sub == h, gates_t, 0.0), axis=0, keepdims=True)
        lf_r = jnp.sum(jnp.where(sub == h + N_HEADS_D, lsig_t, 0.0), axis=0, keepdims=True)
        fh, fm, fl = _split3(jnp.broadcast_to(lf_c, (L, LANE)))
        b_c = (_dot(tril, fh) + _dot(tril, fm) + _dot(tril, fl))[:, 0:1]
        rh, rm, rl = _split3(jnp.broadcast_to(lf_r, (8, L)))
        b_r = (_dot_nt(rh, tril) + _dot_nt(rm, tril) + _dot_nt(rl, tril))[0:1, :]
        g = jnp.sum(lf_c, axis=0, keepdims=True)

        qc = qs_ref[pl.ds(off, L), :]
        kc = ks_ref[pl.ds(off, L), :]
        vc = v_ref[pl.ds(off, L), :]

        dm = jnp.where(lower, b_c - b_r + li_r, NEG)
        inter = b_c + m
        m_t = jnp.maximum(inter, jnp.max(dm, axis=-1, keepdims=True))
        p = jnp.exp(dm - m_t)
        w_inter = jnp.exp(inter - m_t)
        sqk = _dot_nt(qc, kc) * p
        qf = qc.astype(F32)
        num = w_inter * _dot(qc, Ct.astype(BF16)) + _dot(sqk.astype(BF16), vc)
        den = w_inter * jnp.sum(qf * n, axis=-1, keepdims=True) + jnp.sum(sqk, axis=-1, keepdims=True)
        hh = num / jnp.maximum(jnp.abs(den), jnp.exp(-m_t))

        dec_c = g - b_c + li_c
        m_new = jnp.maximum(g + m, jnp.max(dec_c, axis=0, keepdims=True))
        wk = jnp.exp(dec_c - m_new)
        sc = jnp.exp(g + m - m_new)
        kw = kc.astype(F32) * wk
        Ct_new = sc * Ct + _dot_tn(kw.astype(BF16), vc)
        n_new = sc * n + jnp.sum(kw, axis=0, keepdims=True)

        mu = jnp.mean(hh, axis=-1, keepdims=True)
        hc = hh - mu
        var = jnp.mean(hc * hc, axis=-1, keepdims=True)
        hn = hc * lax.rsqrt(var + LN_EPS) * ng
        og = o_ref[pl.ds(off, L), :].astype(F32)
        y_ref[pl.ds(off, L), :] = (jax.nn.sigmoid(og) * hn).astype(y_ref.dtype)
        return Ct_new, n_new, m_new

    init = (jnp.zeros((DQK_D, DV_D), F32), jnp.zeros((1, DQK_D), F32), jnp.zeros((1, 1), F32))
    lax.fori_loop(0, S // L, chunk, init)


def _mlstm(proj, gates, conv_w, conv_b, b_if, norm_g, batch, seq):
    H = N_HEADS_D
    qk0 = 3 * W_C // DQK_D
    v0 = (3 * W_C + 2 * QK_D) // DV_D
    o0 = v0 + H
    return pl.pallas_call(
        _mlstm_kernel,
        out_shape=jax.ShapeDtypeStruct((batch * seq, W_D), BF16),
        grid=(batch, H),
        in_specs=[pl.BlockSpec((seq, DQK_D), lambda b, h: (b, qk0 + h)),
                  pl.BlockSpec((seq, DQK_D), lambda b, h: (b, qk0 + H + h)),
                  pl.BlockSpec((seq, DV_D), lambda b, h: (b, v0 + h)),
                  pl.BlockSpec((seq, DV_D), lambda b, h: (b, o0 + h)),
                  pl.BlockSpec((seq, GATE_PAD), lambda b, h: (b, 0)),
                  pl.BlockSpec((4, DQK_D), lambda b, h: (0, h)),
                  pl.BlockSpec((1, DQK_D), lambda b, h: (0, h)),
                  pl.BlockSpec((4, DQK_D), lambda b, h: (0, H + h)),
                  pl.BlockSpec((1, DQK_D), lambda b, h: (0, H + h)),
                  pl.BlockSpec((1, GATE_PAD), lambda b, h: (0, 0)),
                  pl.BlockSpec((1, DV_D), lambda b, h: (0, h))],
        out_specs=pl.BlockSpec((seq, DV_D), lambda b, h: (b, h)),
        scratch_shapes=[pltpu.VMEM((seq, DQK_D), BF16), pltpu.VMEM((seq, DQK_D), BF16)],
        compiler_params=_cparams(("parallel", "arbitrary")),
        name="mlstm",
    )(proj, proj, proj, proj, gates, conv_w, conv_b, conv_w, conv_b, b_if, norm_g)


def _conv_ffn_block(x, xb, w_up, w_gate, conv_w, conv_b, w_down, ln_g, ln_b, seq):
    pad = D_FF_PAD - D_FF
    wu = jnp.pad(w_up.astype(BF16), ((0, 0), (0, pad)))
    wg = jnp.pad(w_gate.astype(BF16), ((0, 0), (0, pad)))
    wd = jnp.pad(w_down.astype(BF16), ((0, pad), (0, 0)))
    cw = jnp.pad(conv_w, ((0, 0), (0, pad)))
    cb = jnp.pad(conv_b, ((0, pad),)).reshape(1, D_FF_PAD)
    h = _ffn_up(xb, wu, wg, cw, cb, seq)
    z = _ffn_down_resid(h, wd, x)
    return _layer_norm(z, ln_g, ln_b)


def kernel(x, w_in_ab, w_out_ab, w_in_cd, b_if_cd, conv_cd, conv_cd_b, norm_cd_g, w_out_cd,
           ffn_w_up, ffn_w_gate, ffn_conv, ffn_conv_b, ffn_w_down, ln_g, ln_b):
    B, S, D = x.shape
    x = x.reshape(B * S, D)
    xb = x.astype(BF16)

    qkv = _matmul(xb, w_in_ab[0].astype(BF16), BF16)
    ya = _attn_a(qkv, jnp.asarray(_dilated_bias_table()), B, S)
    yb = _attn_b(qkv, jnp.asarray(_moba_mean_matrix(S), BF16), B, S)
    z = _out_proj_resid(ya, yb, w_out_ab[0].astype(BF16), x)
    x, xb = _layer_norm(z, ln_g[0, 0], ln_b[0, 0])
    x, xb = _conv_ffn_block(x, xb, ffn_w_up[0], ffn_w_gate[0], ffn_conv[0], ffn_conv_b[0],
                            ffn_w_down[0], ln_g[0, 1], ln_b[0, 1], S)

    w_in = w_in_cd[0]
    proj = _matmul(xb, w_in[:, :IN_MAIN].astype(BF16), BF16)
    n_gate = 2 * N_HEADS_D
    w_gate = jnp.pad(w_in[:, IN_MAIN:].astype(BF16), ((0, 0), (0, GATE_PAD - n_gate)))
    gates = _matmul(xb, w_gate, F32)
    yc = _attn_c(proj, B, S)
    b_if = jnp.pad(b_if_cd[0], ((0, GATE_PAD - n_gate),)).reshape(1, GATE_PAD)
    yd = _mlstm(proj, gates, conv_cd[0], conv_cd_b[0].reshape(1, 2 * QK_D), b_if,
                norm_cd_g[0].reshape(1, W_D), B, S)
    z = _out_proj_resid(yc, yd, w_out_cd[0].astype(BF16), x)
    x, xb = _layer_norm(z, ln_g[1, 0], ln_b[1, 0])
    x, xb = _conv_ffn_block(x, xb, ffn_w_up[1], ffn_w_gate[1], ffn_conv[1], ffn_conv_b[1],
                            ffn_w_down[1], ln_g[1, 1], ln_b[1, 1], S)
    return x.reshape(B, S, D)
```

```python
import functools

import numpy as np
import jax
import jax.numpy as jnp
from jax import lax
from jax.experimental import pallas as pl
from jax.experimental.pallas import tpu as pltpu

F32 = jnp.float32
BF16 = jnp.bfloat16

D_MODEL = 4096
DEPTH = 2
HEAD_DIM = 128
N_HEADS_A = 16
N_HEADS_B = 16
N_HEADS_C = 16
N_HEADS_D = 8
DQK_D = 128
DV_D = 256
D_FF = 11008
LN_EPS = 1e-5
ALPHA = (2 * DEPTH) ** 0.25
MOBA_BLOCK = 256
MOBA_TOPK = 3
DILATED_PATTERNS = ((128, 1), (512, 4), (2048, 16))

W_A = N_HEADS_A * HEAD_DIM
W_B = N_HEADS_B * HEAD_DIM
W_C = N_HEADS_C * HEAD_DIM
QK_D = N_HEADS_D * DQK_D
W_D = N_HEADS_D * DV_D
IN_MAIN = 12288
GATE_PAD = 128

LANE = 128
ATT_BLK = 256
ATT_HEADS = 4
MLSTM_L = 256
D_FF_PAD = 11264
FFN_TN = 512
NEG = -1e30
SB_UNDERFLOW = 104.0

VMEM_LIMIT = 56 * 1024 * 1024


def _cparams(sem):
    return pltpu.CompilerParams(dimension_semantics=sem, vmem_limit_bytes=VMEM_LIMIT)


def _dot(a, b):
    return jnp.dot(a, b, preferred_element_type=F32)


def _dot_nt(a, b):
    return lax.dot_general(a, b, (((1,), (1,)), ((), ())), preferred_element_type=F32)


def _dot_tn(a, b):
    return lax.dot_general(a, b, (((0,), (0,)), ((), ())), preferred_element_type=F32)


def _split3(x):
    h = x.astype(BF16)
    r = x - h.astype(F32)
    m = r.astype(BF16)
    l = (r - m.astype(F32)).astype(BF16)
    return h, m, l


def _mm_kernel(a_ref, w_ref, o_ref):
    o_ref[...] = _dot(a_ref[...], w_ref[...]).astype(o_ref.dtype)


def _matmul(a, w, out_dtype, tm=1024, tn=1024):
    M, K = a.shape
    N = w.shape[1]
    tn = min(tn, N)
    return pl.pallas_call(
        _mm_kernel,
        out_shape=jax.ShapeDtypeStruct((M, N), out_dtype),
        grid=(M // tm, N // tn),
        in_specs=[pl.BlockSpec((tm, K), lambda i, j: (i, 0)),
                  pl.BlockSpec((K, tn), lambda i, j: (0, j))],
        out_specs=pl.BlockSpec((tm, tn), lambda i, j: (i, j)),
        compiler_params=_cparams(("parallel", "arbitrary")),
        name="matmul",
    )(a, w)


def _mm2_resid_kernel(a1_ref, a2_ref, w1_ref, w2_ref, x_ref, o_ref):
    y = _dot(a1_ref[...], w1_ref[...]) + _dot(a2_ref[...], w2_ref[...])
    o_ref[...] = ALPHA * x_ref[...] + y


def _out_proj_resid(a1, a2, w, x, tm=1024, tn=512):
    M, K1 = a1.shape
    K2 = a2.shape[1]
    assert K1 == K2
    N = w.shape[1]
    return pl.pallas_call(
        _mm2_resid_kernel,
        out_shape=jax.ShapeDtypeStruct((M, N), F32),
        grid=(M // tm, N // tn),
        in_specs=[pl.BlockSpec((tm, K1), lambda i, j: (i, 0)),
                  pl.BlockSpec((tm, K2), lambda i, j: (i, 0)),
                  pl.BlockSpec((K1, tn), lambda i, j: (0, j)),
                  pl.BlockSpec((K2, tn), lambda i, j: (1, j)),
                  pl.BlockSpec((tm, tn), lambda i, j: (i, j))],
        out_specs=pl.BlockSpec((tm, tn), lambda i, j: (i, j)),
        compiler_params=_cparams(("parallel", "arbitrary")),
        name="out_proj_resid",
    )(a1, a2, w, w, x)


def _mmk_resid_kernel(a_ref, w_ref, x_ref, o_ref, acc_ref):
    k = pl.program_id(2)

    @pl.when(k == 0)
    def _():
        acc_ref[...] = _dot(a_ref[...], w_ref[...])

    @pl.when(k > 0)
    def _():
        acc_ref[...] += _dot(a_ref[...], w_ref[...])

    @pl.when(k == pl.num_programs(2) - 1)
    def _():
        o_ref[...] = ALPHA * x_ref[...] + acc_ref[...]


def _ffn_down_resid(h, w, x, tm=1024, tn=1024, tk=2816):
    M, K = h.shape
    N = w.shape[1]
    return pl.pallas_call(
        _mmk_resid_kernel,
        out_shape=jax.ShapeDtypeStruct((M, N), F32),
        grid=(M // tm, N // tn, K // tk),
        in_specs=[pl.BlockSpec((tm, tk), lambda i, j, k: (i, k)),
                  pl.BlockSpec((tk, tn), lambda i, j, k: (k, j)),
                  pl.BlockSpec((tm, tn), lambda i, j, k: (i, j))],
        out_specs=pl.BlockSpec((tm, tn), lambda i, j, k: (i, j)),
        scratch_shapes=[pltpu.VMEM((tm, tn), F32)],
        compiler_params=_cparams(("parallel", "arbitrary", "arbitrary")),
        name="ffn_down_resid",
    )(h, w, x)


def _ln_kernel(z_ref, g_ref, b_ref, o_ref, ob_ref):
    z = z_ref[...]
    mu = jnp.mean(z, axis=-1, keepdims=True)
    zc = z - mu
    var = jnp.mean(zc * zc, axis=-1, keepdims=True)
    y = zc * lax.rsqrt(var + LN_EPS) * g_ref[...] + b_ref[...]
    o_ref[...] = y
    ob_ref[...] = y.astype(BF16)


def _layer_norm(z, g, b, tr=256):
    M, N = z.shape
    return pl.pallas_call(
        _ln_kernel,
        out_shape=(jax.ShapeDtypeStruct((M, N), F32), jax.ShapeDtypeStruct((M, N), BF16)),
        grid=(M // tr,),
        in_specs=[pl.BlockSpec((tr, N), lambda i: (i, 0)),
                  pl.BlockSpec((1, N), lambda i: (0, 0)),
                  pl.BlockSpec((1, N), lambda i: (0, 0))],
        out_specs=(pl.BlockSpec((tr, N), lambda i: (i, 0)),
                   pl.BlockSpec((tr, N), lambda i: (i, 0))),
        compiler_params=_cparams(("parallel",)),
        name="layer_norm",
    )(z, g.reshape(1, N), b.reshape(1, N))


def _ffn_up_kernel(x_ref, wu_ref, wg_ref, cw_ref, cb_ref, h_ref, tail_ref, *, tiles_per_seq):
    i = pl.program_id(1)
    x = x_ref[...]
    u = _dot(x, wu_ref[...])
    g = _dot(x, wg_ref[...])
    tm = g.shape[0]

    @pl.when(i % tiles_per_seq == 0)
    def _():
        tail_ref[...] = jnp.zeros_like(tail_ref)

    tail = tail_ref[...]
    gm1 = tail[7:8, :]
    gm2 = tail[6:7, :]
    row = lax.broadcasted_iota(jnp.int32, g.shape, 0)
    g1 = jnp.where(row == 0, gm1, pltpu.roll(g, 1, 0))
    g2 = jnp.where(row == 0, gm2, jnp.where(row == 1, gm1, pltpu.roll(g, 2, 0)))
    tail_ref[...] = g[tm - 8:, :]
    cw = cw_ref[...]
    gc = cw[0:1, :] * g2 + cw[1:2, :] * g1 + cw[2:3, :] * g + cb_ref[...]
    h_ref[...] = (gc * jax.nn.sigmoid(gc) * u).astype(h_ref.dtype)


def _ffn_up(xb, wu, wg, cw, cb, seq, tm=1024, tn=FFN_TN):
    M, K = xb.shape
    N = wu.shape[1]
    kern = functools.partial(_ffn_up_kernel, tiles_per_seq=seq // tm)
    return pl.pallas_call(
        kern,
        out_shape=jax.ShapeDtypeStruct((M, N), BF16),
        grid=(N // tn, M // tm),
        in_specs=[pl.BlockSpec((tm, K), lambda j, i: (i, 0)),
                  pl.BlockSpec((K, tn), lambda j, i: (0, j)),
                  pl.BlockSpec((K, tn), lambda j, i: (0, j)),
                  pl.BlockSpec((3, tn), lambda j, i: (0, j)),
                  pl.BlockSpec((1, tn), lambda j, i: (0, j))],
        out_specs=pl.BlockSpec((tm, tn), lambda j, i: (i, j)),
        scratch_shapes=[pltpu.VMEM((8, tn), F32)],
        compiler_params=_cparams(("arbitrary", "arbitrary")),
        name="ffn_up",
    )(xb, wu, wg, cw, cb)


def _dilated_bias_table():
    r = np.arange(ATT_BLK)[:, None]
    c = np.arange(ATT_BLK)[None, :]
    tables = []
    for dl in range(4):
        dist = dl * ATT_BLK + r - c
        cnt = np.zeros_like(dist)
        for (w, d) in DILATED_PATTERNS:
            cnt += ((dist >= 0) & (dist % d == 0) & (dist <= w)).astype(dist.dtype)
        with np.errstate(divide="ignore"):
            tables.append(np.where(cnt > 0, np.log(np.maximum(cnt, 1)), NEG))
    return np.stack(tables).astype(np.float32)


def _head_cols(g):
    return slice(g * HEAD_DIM, (g + 1) * HEAD_DIM)


def _softmax_step(q, kb, vb, bias, carry, scale):
    m, l, acc = carry
    s = _dot_nt(q, kb) * scale + bias
    m_new = jnp.maximum(m, jnp.max(s, axis=-1, keepdims=True))
    a = jnp.exp(m - m_new)
    p = jnp.exp(s - m_new)
    l = a * l + jnp.sum(p, axis=-1, keepdims=True)
    acc = a * acc + _dot(p.astype(BF16), vb)
    return m_new, l, acc


def _attn_a_kernel(q_ref, k_ref, v_ref, bias_ref, o_ref, *, scale, heads):
    S = q_ref.shape[0]
    T = ATT_BLK
    for i in range(S // T):
        rows = slice(i * T, (i + 1) * T)
        qs = [q_ref[rows, _head_cols(g)] for g in range(heads)]

        def step(j, bias, carries):
            off = j * T if isinstance(j, int) else pl.multiple_of(j * T, T)
            return tuple(
                _softmax_step(qs[g], k_ref[pl.ds(off, T), _head_cols(g)], v_ref[pl.ds(off, T), _head_cols(g)],
                              bias, carries[g], scale)
                for g in range(heads))

        carries = tuple((jnp.full((T, 1), NEG, F32), jnp.zeros((T, 1), F32), jnp.zeros((T, HEAD_DIM), F32))
                        for _ in range(heads))
        for dl in range(min(3, i + 1)):
            carries = step(i - dl, bias_ref[dl], carries)
        if i >= 3:
            far = bias_ref[3]
            carries = lax.fori_loop(0, i - 2, lambda j, c: step(j, far, c), carries)
        for g in range(heads):
            m, l, acc = carries[g]
            o_ref[rows, _head_cols(g)] = (acc / l).astype(o_ref.dtype)


def _attn_a(qkv, bias, batch, seq):
    G = ATT_HEADS
    kern = functools.partial(_attn_a_kernel, scale=HEAD_DIM ** -0.5, heads=G)
    HB = N_HEADS_A // G
    W = G * HEAD_DIM
    return pl.pallas_call(
        kern,
        out_shape=jax.ShapeDtypeStruct((batch * seq, W_A), BF16),
        grid=(batch, HB),
        in_specs=[pl.BlockSpec((seq, W), lambda b, h: (b, h)),
                  pl.BlockSpec((seq, W), lambda b, h: (b, HB + h)),
                  pl.BlockSpec((seq, W), lambda b, h: (b, 2 * HB + h)),
                  pl.BlockSpec((4, ATT_BLK, ATT_BLK), lambda b, h: (0, 0, 0))],
        out_specs=pl.BlockSpec((seq, W), lambda b, h: (b, h)),
        compiler_params=_cparams(("parallel", "parallel")),
        name="attn_dilated",
    )(qkv, qkv, qkv, bias)


def _moba_mean_matrix(seq):
    m = np.zeros((LANE, seq), np.float32)
    for j in range(seq // MOBA_BLOCK):
        m[j, j * MOBA_BLOCK:(j + 1) * MOBA_BLOCK] = 1.0 / MOBA_BLOCK
    return m


def _moba_select_bias(q, kmean3, i, lane):
    km_h, km_m, km_l = kmean3
    valid = lane < i
    if i > MOBA_TOPK:
        gate = _dot_nt(q, km_h) + _dot_nt(q, km_m) + _dot_nt(q, km_l)
        rank = jnp.zeros(gate.shape, jnp.int32)
        for jp in range(i):
            gj = gate[:, jp:jp + 1]
            beats = (gj > gate) | ((gj == gate) & (jp < lane))
            rank = rank + beats.astype(jnp.int32)
        sel = valid & (rank < MOBA_TOPK)
    else:
        sel = valid
    return jnp.where(sel, 0.0, NEG)


def _attn_b_kernel(q_ref, k_ref, v_ref, mean_ref, o_ref, *, scale, heads):
    S = q_ref.shape[0]
    T = ATT_BLK
    kmeans = [_split3(_dot(mean_ref[...], k_ref[:, _head_cols(g)])) for g in range(heads)]
    lane = lax.broadcasted_iota(jnp.int32, (T, LANE), 1)
    row = lax.broadcasted_iota(jnp.int32, (T, T), 0)
    col = lax.broadcasted_iota(jnp.int32, (T, T), 1)
    causal_bias = jnp.where(col <= row, 0.0, NEG)

    for i in range(S // T):
        rows = slice(i * T, (i + 1) * T)
        qs = [q_ref[rows, _head_cols(g)] for g in range(heads)]
        init = (jnp.full((T, 1), NEG, F32), jnp.zeros((T, 1), F32), jnp.zeros((T, HEAD_DIM), F32))
        carries = tuple(
            _softmax_step(qs[g], k_ref[rows, _head_cols(g)], v_ref[rows, _head_cols(g)], causal_bias, init, scale)
            for g in range(heads))
        if i > 0:
            selbias = [_moba_select_bias(qs[g], kmeans[g], i, lane) for g in range(heads)]

            def step(j, carries):
                off = pl.multiple_of(j * T, T)
                out = []
                for g in range(heads):
                    bj = jnp.sum(jnp.where(lane == j, selbias[g], 0.0), axis=-1, keepdims=True)
                    out.append(_softmax_step(qs[g], k_ref[pl.ds(off, T), _head_cols(g)],
                                             v_ref[pl.ds(off, T), _head_cols(g)], bj, carries[g], scale))
                return tuple(out)

            carries = lax.fori_loop(0, i, step, carries)
        for g in range(heads):
            m, l, acc = carries[g]
            o_ref[rows, _head_cols(g)] = (acc / l).astype(o_ref.dtype)


def _attn_b(qkv, mean_mat, batch, seq):
    G = ATT_HEADS
    kern = functools.partial(_attn_b_kernel, scale=HEAD_DIM ** -0.5, heads=G)
    HB = N_HEADS_B // G
    W = G * HEAD_DIM
    base = 3 * N_HEADS_A // G
    return pl.pallas_call(
        kern,
        out_shape=jax.ShapeDtypeStruct((batch * seq, W_B), BF16),
        grid=(batch, HB),
        in_specs=[pl.BlockSpec((seq, W), lambda b, h: (b, base + h)),
                  pl.BlockSpec((seq, W), lambda b, h: (b, base + HB + h)),
                  pl.BlockSpec((seq, W), lambda b, h: (b, base + 2 * HB + h)),
                  pl.BlockSpec((LANE, seq), lambda b, h: (0, 0))],
        out_specs=pl.BlockSpec((seq, W), lambda b, h: (b, h)),
        compiler_params=_cparams(("parallel", "parallel")),
        name="attn_moba",
    )(qkv, qkv, qkv, mean_mat)


def _attn_c_kernel(q_ref, k_ref, v_ref, o_ref, *, scale, heads):
    S = q_ref.shape[0]
    T = ATT_BLK
    row = lax.broadcasted_iota(jnp.int32, (T, T), 0)
    col = lax.broadcasted_iota(jnp.int32, (T, T), 1)
    tri = jnp.where(row >= col, 1.0, 0.0).astype(BF16)
    strict = col < row

    def block(q, kb, vb, carry, diag):
        c, acc = carry
        z = _dot_nt(q, kb) * scale
        sp = jnp.maximum(z, 0.0) + jnp.log(1.0 + jnp.exp(-jnp.abs(z)))
        if diag:
            sp = jnp.where(strict, sp, 0.0)
        sp_h = sp.astype(BF16)
        sp_l = (sp - sp_h.astype(F32)).astype(BF16)
        rc_in = _dot(sp_h, tri) + _dot(sp_l, tri)
        a = jnp.exp(z - (rc_in + c))
        if diag:
            a = jnp.where(strict, a, 0.0)
        acc = acc + _dot(a.astype(BF16), vb)
        c = c + rc_in[:, 0:1]
        return c, acc

    for i in range(S // T):
        rows = slice(i * T, (i + 1) * T)
        qs = [q_ref[rows, _head_cols(g)] for g in range(heads)]
        init = (jnp.zeros((T, 1), F32), jnp.zeros((T, HEAD_DIM), F32))
        carries = tuple(block(qs[g], k_ref[rows, _head_cols(g)], v_ref[rows, _head_cols(g)], init, True)
                        for g in range(heads))
        if i > 0:
            def cond(state):
                return jnp.logical_and(state[0] < i, state[1] > 0)

            def step(state):
                t, _, carries = state
                off = pl.multiple_of((i - 1 - t) * T, T)
                new = tuple(block(qs[g], k_ref[pl.ds(off, T), _head_cols(g)],
                                  v_ref[pl.ds(off, T), _head_cols(g)], carries[g], False)
                            for g in range(heads))
                cmin = new[0][0]
                for g in range(1, heads):
                    cmin = jnp.minimum(cmin, new[g][0])
                more = (jnp.min(cmin) < SB_UNDERFLOW).astype(jnp.int32)
                return t + 1, more, new

            _, _, carries = lax.while_loop(cond, step, (jnp.int32(0), jnp.int32(1), carries))
        for g in range(heads):
            o_ref[rows, _head_cols(g)] = carries[g][1].astype(o_ref.dtype)


def _attn_c(proj, batch, seq):
    G = ATT_HEADS
    kern = functools.partial(_attn_c_kernel, scale=HEAD_DIM ** -0.5, heads=G)
    HB = N_HEADS_C // G
    W = G * HEAD_DIM
    return pl.pallas_call(
        kern,
        out_shape=jax.ShapeDtypeStruct((batch * seq, W_C), BF16),
        grid=(batch, HB),
        in_specs=[pl.BlockSpec((seq, W), lambda b, h: (b, h)),
                  pl.BlockSpec((seq, W), lambda b, h: (b, HB + h)),
                  pl.BlockSpec((seq, W), lambda b, h: (b, 2 * HB + h))],
        out_specs=pl.BlockSpec((seq, W), lambda b, h: (b, h)),
        compiler_params=_cparams(("parallel", "parallel")),
        name="attn_stickbreak",
    )(proj, proj, proj)


def _conv4_silu(x, cw, cb):
    row = lax.broadcasted_iota(jnp.int32, x.shape, 0)
    y = cw[3:4, :] * x + cb
    for d in (1, 2, 3):
        xs = jnp.where(row >= d, pltpu.roll(x, d, 0), 0.0)
        y = y + cw[3 - d:4 - d, :] * xs
    return y * jax.nn.sigmoid(y)


def _mlstm_kernel(q_ref, k_ref, v_ref, o_ref, g_ref, cwq_ref, cbq_ref, cwk_ref, cbk_ref,
                  bif_ref, ng_ref, y_ref, qs_ref, ks_ref):
    h = pl.program_id(1)
    S = q_ref.shape[0]
    L = MLSTM_L
    qs_ref[...] = _conv4_silu(q_ref[...].astype(F32), cwq_ref[...], cbq_ref[...]).astype(BF16)
    ks_ref[...] = (_conv4_silu(k_ref[...].astype(F32), cwk_ref[...], cbk_ref[...])
                   * (DQK_D ** -0.5)).astype(BF16)

    row = lax.broadcasted_iota(jnp.int32, (L, L), 0)
    col = lax.broadcasted_iota(jnp.int32, (L, L), 1)
    lower = row >= col
    tril = jnp.where(lower, 1.0, 0.0).astype(BF16)
    lane = lax.broadcasted_iota(jnp.int32, (L, LANE), 1)
    sub = lax.broadcasted_iota(jnp.int32, (LANE, L), 0)
    ng = ng_ref[...]

    def chunk(ci, carry):
        Ct, n, m = carry
        off = pl.multiple_of(ci * L, L)
        gates = g_ref[pl.ds(off, L), :] + bif_ref[...]
        lsig = jnp.minimum(gates, 0.0) - jnp.log(1.0 + jnp.exp(-jnp.abs(gates)))
        li_c = jnp.sum(jnp.where(lane == h, gates, 0.0), axis=-1, keepdims=True)
        lf_c = jnp.sum(jnp.where(lane == h + N_HEADS_D, lsig, 0.0), axis=-1, keepdims=True)
        gates_t = gates.T
        lsig_t = lsig.T
        li_r = jnp.sum(jnp.where(sub == h, gates_t, 0.0), axis=0, keepdims=True)
        lf_r = jnp.sum(jnp.where(sub == h + N_HEADS_D, lsig_t, 0.0), axis=0, keepdims=True)
        fh, fm, fl = _split3(jnp.broadcast_to(lf_c, (L, LANE)))
        b_c = (_dot(tril, fh) + _dot(tril, fm) + _dot(tril, fl))[:, 0:1]
        rh, rm, rl = _split3(jnp.broadcast_to(lf_r, (8, L)))
        b_r = (_dot_nt(rh, tril) + _dot_nt(rm, tril) + _dot_nt(rl, tril))[0:1, :]
        g = jnp.sum(lf_c, axis=0, keepdims=True)

        qc = qs_ref[pl.ds(off, L), :]
        kc = ks_ref[pl.ds(off, L), :]
        vc = v_ref[pl.ds(off, L), :]

        dm = jnp.where(lower, b_c - b_r + li_r, NEG)
        inter = b_c + m
        m_t = jnp.maximum(inter, jnp.max(dm, axis=-1, keepdims=True))
        p = jnp.exp(dm - m_t)
        w_inter = jnp.exp(inter - m_t)
        sqk = _dot_nt(qc, kc) * p
        qf = qc.astype(F32)
        num = w_inter * _dot(qc, Ct.astype(BF16)) + _dot(sqk.astype(BF16), vc)
        den = w_inter * jnp.sum(qf * n, axis=-1, keepdims=True) + jnp.sum(sqk, axis=-1, keepdims=True)
        hh = num / jnp.maximum(jnp.abs(den), jnp.exp(-m_t))

        dec_c = g - b_c + li_c
        m_new = jnp.maximum(g + m, jnp.max(dec_c, axis=0, keepdims=True))
        wk = jnp.exp(dec_c - m_new)
        sc = jnp.exp(g + m - m_new)
        kw = kc.astype(F32) * wk
        Ct_new = sc * Ct + _dot_tn(kw.astype(BF16), vc)
        n_new = sc * n + jnp.sum(kw, axis=0, keepdims=True)

        mu = jnp.mean(hh, axis=-1, keepdims=True)
        hc = hh - mu
        var = jnp.mean(hc * hc, axis=-1, keepdims=True)
        hn = hc * lax.rsqrt(var + LN_EPS) * ng
        og = o_ref[pl.ds(off, L), :].astype(F32)
        y_ref[pl.ds(off, L), :] = (jax.nn.sigmoid(og) * hn).astype(y_ref.dtype)
        return Ct_new, n_new, m_new

    init = (jnp.zeros((DQK_D, DV_D), F32), jnp.zeros((1, DQK_D), F32), jnp.zeros((1, 1), F32))
    lax.fori_loop(0, S // L, chunk, init)


def _mlstm(proj, gates, conv_w, conv_b, b_if, norm_g, batch, seq):
    H = N_HEADS_D
    qk0 = 3 * W_C // DQK_D
    v0 = (3 * W_C + 2 * QK_D) // DV_D
    o0 = v0 + H
    return pl.pallas_call(
        _mlstm_kernel,
        out_shape=jax.ShapeDtypeStruct((batch * seq, W_D), BF16),
        grid=(batch, H),
        in_specs=[pl.BlockSpec((seq, DQK_D), lambda b, h: (b, qk0 + h)),
                  pl.BlockSpec((seq, DQK_D), lambda b, h: (b, qk0 + H + h)),
                  pl.BlockSpec((seq, DV_D), lambda b, h: (b, v0 + h)),
                  pl.BlockSpec((seq, DV_D), lambda b, h: (b, o0 + h)),
                  pl.BlockSpec((seq, GATE_PAD), lambda b, h: (b, 0)),
                  pl.BlockSpec((4, DQK_D), lambda b, h: (0, h)),
                  pl.BlockSpec((1, DQK_D), lambda b, h: (0, h)),
                  pl.BlockSpec((4, DQK_D), lambda b, h: (0, H + h)),
                  pl.BlockSpec((1, DQK_D), lambda b, h: (0, H + h)),
                  pl.BlockSpec((1, GATE_PAD), lambda b, h: (0, 0)),
                  pl.BlockSpec((1, DV_D), lambda b, h: (0, h))],
        out_specs=pl.BlockSpec((seq, DV_D), lambda b, h: (b, h)),
        scratch_shapes=[pltpu.VMEM((seq, DQK_D), BF16), pltpu.VMEM((seq, DQK_D), BF16)],
        compiler_params=_cparams(("parallel", "arbitrary")),
        name="mlstm",
    )(proj, proj, proj, proj, gates, conv_w, conv_b, conv_w, conv_b, b_if, norm_g)


def _conv_ffn_block(x, xb, w_up, w_gate, conv_w, conv_b, w_down, ln_g, ln_b, seq):
    pad = D_FF_PAD - D_FF
    wu = jnp.pad(w_up.astype(BF16), ((0, 0), (0, pad)))
    wg = jnp.pad(w_gate.astype(BF16), ((0, 0), (0, pad)))
    wd = jnp.pad(w_down.astype(BF16), ((0, pad), (0, 0)))
    cw = jnp.pad(conv_w, ((0, 0), (0, pad)))
    cb = jnp.pad(conv_b, ((0, pad),)).reshape(1, D_FF_PAD)
    h = _ffn_up(xb, wu, wg, cw, cb, seq)
    z = _ffn_down_resid(h, wd, x)
    return _layer_norm(z, ln_g, ln_b)


def kernel(x, w_in_ab, w_out_ab, w_in_cd, b_if_cd, conv_cd, conv_cd_b, norm_cd_g, w_out_cd,
           ffn_w_up, ffn_w_gate, ffn_conv, ffn_conv_b, ffn_w_down, ln_g, ln_b):
    B, S, D = x.shape
    x = x.reshape(B * S, D)
    xb = x.astype(BF16)

    qkv = _matmul(xb, w_in_ab[0].astype(BF16), BF16)
    ya = _attn_a(qkv, jnp.asarray(_dilated_bias_table()), B, S)
    yb = _attn_b(qkv, jnp.asarray(_moba_mean_matrix(S), BF16), B, S)
    z = _out_proj_resid(ya, yb, w_out_ab[0].astype(BF16), x)
    x, xb = _layer_norm(z, ln_g[0, 0], ln_b[0, 0])
    x, xb = _conv_ffn_block(x, xb, ffn_w_up[0], ffn_w_gate[0], ffn_conv[0], ffn_conv_b[0],
                            ffn_w_down[0], ln_g[0, 1], ln_b[0, 1], S)

    w_in = w_in_cd[0]
    proj = _matmul(xb, w_in[:, :IN_MAIN].astype(BF16), BF16)
    n_gate = 2 * N_HEADS_D
    w_gate = jnp.pad(w_in[:, IN_MAIN:].astype(BF16), ((0, 0), (0, GATE_PAD - n_gate)))
    gates = _matmul(xb, w_gate, F32)
    yc = _attn_c(proj, B, S)
    b_if = jnp.pad(b_if_cd[0], ((0, GATE_PAD - n_gate),)).reshape(1, GATE_PAD)
    yd = _mlstm(proj, gates, conv_cd[0], conv_cd_b[0].reshape(1, 2 * QK_D), b_if,
                norm_cd_g[0].reshape(1, W_D), B, S)
    z = _out_proj_resid(yc, yd, w_out_cd[0].astype(BF16), x)
    x, xb = _layer_norm(z, ln_g[1, 0], ln_b[1, 0])
    x, xb = _conv_ffn_block(x, xb, ffn_w_up[1], ffn_w_gate[1], ffn_conv[1], ffn_conv_b[1],
                            ffn_w_down[1], ln_g[1, 1], ln_b[1, 1], S)
    return x.reshape(B, S, D)
```

```python
import functools

import numpy as np
import jax
import jax.numpy as jnp
from jax import lax
from jax.experimental import pallas as pl
from jax.experimental.pallas import tpu as pltpu

F32 = jnp.float32
BF16 = jnp.bfloat16

D_MODEL = 4096
DEPTH = 2
HEAD_DIM = 128
N_HEADS_A = 16
N_HEADS_B = 16
N_HEADS_C = 16
N_HEADS_D = 8
DQK_D = 128
DV_D = 256
D_FF = 11008
LN_EPS = 1e-5
ALPHA = (2 * DEPTH) ** 0.25
MOBA_BLOCK = 256
MOBA_TOPK = 3
DILATED_PATTERNS = ((128, 1), (512, 4), (2048, 16))

W_A = N_HEADS_A * HEAD_DIM
W_B = N_HEADS_B * HEAD_DIM
W_C = N_HEADS_C * HEAD_DIM
QK_D = N_HEADS_D * DQK_D
W_D = N_HEADS_D * DV_D
IN_MAIN = 12288
GATE_PAD = 128

LANE = 128
ATT_BLK = 256
ATT_HEADS = 4
MLSTM_L = 256
FFN_TN = 256
NEG = -1e30
SB_UNDERFLOW = 104.0

VMEM_LIMIT = 56 * 1024 * 1024


def _cparams(sem):
    return pltpu.CompilerParams(dimension_semantics=sem, vmem_limit_bytes=VMEM_LIMIT)


def _dot(a, b):
    return jnp.dot(a, b, preferred_element_type=F32)


def _dot_nt(a, b):
    return lax.dot_general(a, b, (((1,), (1,)), ((), ())), preferred_element_type=F32)


def _dot_tn(a, b):
    return lax.dot_general(a, b, (((0,), (0,)), ((), ())), preferred_element_type=F32)


def _split3(x):
    h = x.astype(BF16)
    r = x - h.astype(F32)
    m = r.astype(BF16)
    l = (r - m.astype(F32)).astype(BF16)
    return h, m, l


def _mm_wres_kernel(a_ref, w_ref, o_ref, wb_ref):
    @pl.when(pl.program_id(1) == 0)
    def _():
        wb_ref[...] = w_ref[...].astype(BF16)

    o_ref[...] = _dot(a_ref[...], wb_ref[...]).astype(o_ref.dtype)


def _matmul(a, w, n_cols, out_dtype, tm=1024, tn=512):
    M, K = a.shape
    tn = min(tn, n_cols)
    return pl.pallas_call(
        _mm_wres_kernel,
        out_shape=jax.ShapeDtypeStruct((M, n_cols), out_dtype),
        grid=(n_cols // tn, M // tm),
        in_specs=[pl.BlockSpec((tm, K), lambda j, i: (i, 0)),
                  pl.BlockSpec((K, tn), lambda j, i: (0, j))],
        out_specs=pl.BlockSpec((tm, tn), lambda j, i: (i, j)),
        scratch_shapes=[pltpu.VMEM((K, tn), BF16)],
        compiler_params=_cparams(("arbitrary", "arbitrary")),
        name="matmul",
    )(a, w)


def _mm2_resid_kernel(a1_ref, a2_ref, w1_ref, w2_ref, x_ref, o_ref, wb1_ref, wb2_ref):
    @pl.when(pl.program_id(1) == 0)
    def _():
        wb1_ref[...] = w1_ref[...].astype(BF16)
        wb2_ref[...] = w2_ref[...].astype(BF16)

    y = _dot(a1_ref[...], wb1_ref[...]) + _dot(a2_ref[...], wb2_ref[...])
    o_ref[...] = ALPHA * x_ref[...] + y


def _out_proj_resid(a1, a2, w, x, tm=1024, tn=512):
    M, K1 = a1.shape
    K2 = a2.shape[1]
    assert K1 == K2
    N = w.shape[1]
    return pl.pallas_call(
        _mm2_resid_kernel,
        out_shape=jax.ShapeDtypeStruct((M, N), F32),
        grid=(N // tn, M // tm),
        in_specs=[pl.BlockSpec((tm, K1), lambda j, i: (i, 0)),
                  pl.BlockSpec((tm, K2), lambda j, i: (i, 0)),
                  pl.BlockSpec((K1, tn), lambda j, i: (0, j)),
                  pl.BlockSpec((K2, tn), lambda j, i: (1, j)),
                  pl.BlockSpec((tm, tn), lambda j, i: (i, j))],
        out_specs=pl.BlockSpec((tm, tn), lambda j, i: (i, j)),
        scratch_shapes=[pltpu.VMEM((K1, tn), BF16), pltpu.VMEM((K2, tn), BF16)],
        compiler_params=_cparams(("arbitrary", "arbitrary")),
        name="out_proj_resid",
    )(a1, a2, w, w, x)


def _mm_resid_kernel(a_ref, w_ref, x_ref, o_ref):
    o_ref[...] = ALPHA * x_ref[...] + _dot(a_ref[...], w_ref[...])


def _ffn_down_resid(h, w, x, tm=512, tn=512):
    M, K = h.shape
    N = w.shape[1]
    return pl.pallas_call(
        _mm_resid_kernel,
        out_shape=jax.ShapeDtypeStruct((M, N), F32),
        grid=(M // tm, N // tn),
        in_specs=[pl.BlockSpec((tm, K), lambda i, j: (i, 0)),
                  pl.BlockSpec((K, tn), lambda i, j: (0, j)),
                  pl.BlockSpec((tm, tn), lambda i, j: (i, j))],
        out_specs=pl.BlockSpec((tm, tn), lambda i, j: (i, j)),
        compiler_params=_cparams(("parallel", "arbitrary")),
        name="ffn_down_resid",
    )(h, w, x)


def _ln_kernel(z_ref, g_ref, b_ref, o_ref, ob_ref):
    z = z_ref[...]
    mu = jnp.mean(z, axis=-1, keepdims=True)
    zc = z - mu
    var = jnp.mean(zc * zc, axis=-1, keepdims=True)
    y = zc * lax.rsqrt(var + LN_EPS) * g_ref[...] + b_ref[...]
    o_ref[...] = y
    ob_ref[...] = y.astype(BF16)


def _layer_norm(z, g, b, tr=256):
    M, N = z.shape
    return pl.pallas_call(
        _ln_kernel,
        out_shape=(jax.ShapeDtypeStruct((M, N), F32), jax.ShapeDtypeStruct((M, N), BF16)),
        grid=(M // tr,),
        in_specs=[pl.BlockSpec((tr, N), lambda i: (i, 0)),
                  pl.BlockSpec((1, N), lambda i: (0, 0)),
                  pl.BlockSpec((1, N), lambda i: (0, 0))],
        out_specs=(pl.BlockSpec((tr, N), lambda i: (i, 0)),
                   pl.BlockSpec((tr, N), lambda i: (i, 0))),
        compiler_params=_cparams(("parallel",)),
        name="layer_norm",
    )(z, g.reshape(1, N), b.reshape(1, N))


def _ffn_up_kernel(x_ref, wu_ref, wg_ref, cw_ref, cb_ref, h_ref, wub_ref, wgb_ref):
    @pl.when(pl.program_id(1) == 0)
    def _():
        wub_ref[...] = wu_ref[...].astype(BF16)
        wgb_ref[...] = wg_ref[...].astype(BF16)

    x = x_ref[...]
    u = _dot(x, wub_ref[...])
    g = _dot(x, wgb_ref[...])
    row = lax.broadcasted_iota(jnp.int32, g.shape, 0)
    g1 = jnp.where(row >= 1, pltpu.roll(g, 1, 0), 0.0)
    g2 = jnp.where(row >= 2, pltpu.roll(g, 2, 0), 0.0)
    cw = cw_ref[...]
    gc = cw[0:1, :] * g2 + cw[1:2, :] * g1 + cw[2:3, :] * g + cb_ref[...]
    h_ref[...] = (gc * jax.nn.sigmoid(gc) * u).astype(h_ref.dtype)


def _ffn_up(xb, wu, wg, cw, cb, seq, tn=FFN_TN):
    M, K = xb.shape
    N = wu.shape[1]
    once = pl.Buffered(1)
    return pl.pallas_call(
        _ffn_up_kernel,
        out_shape=jax.ShapeDtypeStruct((M, N), BF16),
        grid=(N // tn, M // seq),
        in_specs=[pl.BlockSpec((seq, K), lambda j, i: (i, 0)),
                  pl.BlockSpec((K, tn), lambda j, i: (0, j), pipeline_mode=once),
                  pl.BlockSpec((K, tn), lambda j, i: (0, j), pipeline_mode=once),
                  pl.BlockSpec((3, tn), lambda j, i: (0, j)),
                  pl.BlockSpec((1, tn), lambda j, i: (0, j))],
        out_specs=pl.BlockSpec((seq, tn), lambda j, i: (i, j)),
        scratch_shapes=[pltpu.VMEM((K, tn), BF16), pltpu.VMEM((K, tn), BF16)],
        compiler_params=_cparams(("arbitrary", "arbitrary")),
        name="ffn_up",
    )(xb, wu, wg, cw, cb)


def _dilated_bias_table():
    r = np.arange(ATT_BLK)[:, None]
    c = np.arange(ATT_BLK)[None, :]
    tables = []
    for dl in range(4):
        dist = dl * ATT_BLK + r - c
        cnt = np.zeros_like(dist)
        for (w, d) in DILATED_PATTERNS:
            cnt += ((dist >= 0) & (dist % d == 0) & (dist <= w)).astype(dist.dtype)
        with np.errstate(divide="ignore"):
            tables.append(np.where(cnt > 0, np.log(np.maximum(cnt, 1)), NEG))
    return np.stack(tables).astype(np.float32)


def _head_cols(g):
    return slice(g * HEAD_DIM, (g + 1) * HEAD_DIM)


def _softmax_step(q, kb, vb, bias, carry, scale):
    m, l, acc = carry
    s = _dot_nt(q, kb) * scale + bias
    m_new = jnp.maximum(m, jnp.max(s, axis=-1, keepdims=True))
    a = jnp.exp(m - m_new)
    p = jnp.exp(s - m_new)
    l = a * l + jnp.sum(p, axis=-1, keepdims=True)
    acc = a * acc + _dot(p.astype(BF16), vb)
    return m_new, l, acc


def _attn_a_kernel(q_ref, k_ref, v_ref, bias_ref, o_ref, *, scale, heads):
    S = q_ref.shape[0]
    T = ATT_BLK
    for i in range(S // T):
        rows = slice(i * T, (i + 1) * T)
        qs = [q_ref[rows, _head_cols(g)] for g in range(heads)]

        def step(j, bias, carries):
            off = j * T if isinstance(j, int) else pl.multiple_of(j * T, T)
            return tuple(
                _softmax_step(qs[g], k_ref[pl.ds(off, T), _head_cols(g)], v_ref[pl.ds(off, T), _head_cols(g)],
                              bias, carries[g], scale)
                for g in range(heads))

        carries = tuple((jnp.full((T, 1), NEG, F32), jnp.zeros((T, 1), F32), jnp.zeros((T, HEAD_DIM), F32))
                        for _ in range(heads))
        for dl in range(min(3, i + 1)):
            carries = step(i - dl, bias_ref[dl], carries)
        if i >= 3:
            far = bias_ref[3]
            carries = lax.fori_loop(0, i - 2, lambda j, c: step(j, far, c), carries)
        for g in range(heads):
            m, l, acc = carries[g]
            o_ref[rows, _head_cols(g)] = (acc / l).astype(o_ref.dtype)


def _attn_a(qkv, bias, batch, seq):
    G = ATT_HEADS
    kern = functools.partial(_attn_a_kernel, scale=HEAD_DIM ** -0.5, heads=G)
    HB = N_HEADS_A // G
    W = G * HEAD_DIM
    return pl.pallas_call(
        kern,
        out_shape=jax.ShapeDtypeStruct((batch * seq, W_A), BF16),
        grid=(batch, HB),
        in_specs=[pl.BlockSpec((seq, W), lambda b, h: (b, h)),
                  pl.BlockSpec((seq, W), lambda b, h: (b, HB + h)),
                  pl.BlockSpec((seq, W), lambda b, h: (b, 2 * HB + h)),
                  pl.BlockSpec((4, ATT_BLK, ATT_BLK), lambda b, h: (0, 0, 0))],
        out_specs=pl.BlockSpec((seq, W), lambda b, h: (b, h)),
        compiler_params=_cparams(("parallel", "parallel")),
        name="attn_dilated",
    )(qkv, qkv, qkv, bias)


def _moba_mean_matrix(seq):
    m = np.zeros((LANE, seq), np.float32)
    for j in range(seq // MOBA_BLOCK):
        m[j, j * MOBA_BLOCK:(j + 1) * MOBA_BLOCK] = 1.0 / MOBA_BLOCK
    return m


def _moba_select_bias(q, kmean3, i, lane):
    km_h, km_m, km_l = kmean3
    valid = lane < i
    if i > MOBA_TOPK:
        gate = _dot_nt(q, km_h) + _dot_nt(q, km_m) + _dot_nt(q, km_l)
        rank = jnp.zeros(gate.shape, jnp.int32)
        for jp in range(i):
            gj = gate[:, jp:jp + 1]
            beats = (gj > gate) | ((gj == gate) & (jp < lane))
            rank = rank + beats.astype(jnp.int32)
        sel = valid & (rank < MOBA_TOPK)
    else:
        sel = valid
    return jnp.where(sel, 0.0, NEG)


def _attn_b_kernel(q_ref, k_ref, v_ref, mean_ref, o_ref, *, scale, heads):
    S = q_ref.shape[0]
    T = ATT_BLK
    kmeans = [_split3(_dot(mean_ref[...], k_ref[:, _head_cols(g)])) for g in range(heads)]
    lane = lax.broadcasted_iota(jnp.int32, (T, LANE), 1)
    row = lax.broadcasted_iota(jnp.int32, (T, T), 0)
    col = lax.broadcasted_iota(jnp.int32, (T, T), 1)
    causal_bias = jnp.where(col <= row, 0.0, NEG)

    for i in range(S // T):
        rows = slice(i * T, (i + 1) * T)
        qs = [q_ref[rows, _head_cols(g)] for g in range(heads)]
        init = (jnp.full((T, 1), NEG, F32), jnp.zeros((T, 1), F32), jnp.zeros((T, HEAD_DIM), F32))
        carries = tuple(
            _softmax_step(qs[g], k_ref[rows, _head_cols(g)], v_ref[rows, _head_cols(g)], causal_bias, init, scale)
            for g in range(heads))
        if i > 0:
            selbias = [_moba_select_bias(qs[g], kmeans[g], i, lane) for g in range(heads)]

            def step(j, carries):
                off = pl.multiple_of(j * T, T)
                out = []
                for g in range(heads):
                    bj = jnp.sum(jnp.where(lane == j, selbias[g], 0.0), axis=-1, keepdims=True)
                    out.append(_softmax_step(qs[g], k_ref[pl.ds(off, T), _head_cols(g)],
                                             v_ref[pl.ds(off, T), _head_cols(g)], bj, carries[g], scale))
                return tuple(out)

            carries = lax.fori_loop(0, i, step, carries)
        for g in range(heads):
            m, l, acc = carries[g]
            o_ref[rows, _head_cols(g)] = (acc / l).astype(o_ref.dtype)


def _attn_b(qkv, mean_mat, batch, seq):
    G = ATT_HEADS
    kern = functools.partial(_attn_b_kernel, scale=HEAD_DIM ** -0.5, heads=G)
    HB = N_HEADS_B // G
    W = G * HEAD_DIM
    base = 3 * N_HEADS_A // G
    return pl.pallas_call(
        kern,
        out_shape=jax.ShapeDtypeStruct((batch * seq, W_B), BF16),
        grid=(batch, HB),
        in_specs=[pl.BlockSpec((seq, W), lambda b, h: (b, base + h)),
                  pl.BlockSpec((seq, W), lambda b, h: (b, base + HB + h)),
                  pl.BlockSpec((seq, W), lambda b, h: (b, base + 2 * HB + h)),
                  pl.BlockSpec((LANE, seq), lambda b, h: (0, 0))],
        out_specs=pl.BlockSpec((seq, W), lambda b, h: (b, h)),
        compiler_params=_cparams(("parallel", "parallel")),
        name="attn_moba",
    )(qkv, qkv, qkv, mean_mat)


def _attn_c_kernel(q_ref, k_ref, v_ref, o_ref, *, scale, heads):
    S = q_ref.shape[0]
    T = ATT_BLK
    row = lax.broadcasted_iota(jnp.int32, (T, T), 0)
    col = lax.broadcasted_iota(jnp.int32, (T, T), 1)
    tri = jnp.where(row >= col, 1.0, 0.0).astype(BF16)
    strict = col < row

    def block(q, kb, vb, carry, diag):
        c, acc = carry
        z = _dot_nt(q, kb) * scale
        sp = jnp.maximum(z, 0.0) + jnp.log(1.0 + jnp.exp(-jnp.abs(z)))
        if diag:
            sp = jnp.where(strict, sp, 0.0)
        sp_h = sp.astype(BF16)
        sp_l = (sp - sp_h.astype(F32)).astype(BF16)
        rc_in = _dot(sp_h, tri) + _dot(sp_l, tri)
        a = jnp.exp(z - (rc_in + c))
        if diag:
            a = jnp.where(strict, a, 0.0)
        acc = acc + _dot(a.astype(BF16), vb)
        c = c + rc_in[:, 0:1]
        return c, acc

    for i in range(S // T):
        rows = slice(i * T, (i + 1) * T)
        qs = [q_ref[rows, _head_cols(g)] for g in range(heads)]
        init = (jnp.zeros((T, 1), F32), jnp.zeros((T, HEAD_DIM), F32))
        carries = tuple(block(qs[g], k_ref[rows, _head_cols(g)], v_ref[rows, _head_cols(g)], init, True)
                        for g in range(heads))
        if i > 0:
            def cond(state):
                return jnp.logical_and(state[0] < i, state[1] > 0)

            def step(state):
                t, _, carries = state
                off = pl.multiple_of((i - 1 - t) * T, T)
                new = tuple(block(qs[g], k_ref[pl.ds(off, T), _head_cols(g)],
                                  v_ref[pl.ds(off, T), _head_cols(g)], carries[g], False)
                            for g in range(heads))
                cmin = new[0][0]
                for g in range(1, heads):
                    cmin = jnp.minimum(cmin, new[g][0])
                more = (jnp.min(cmin) < SB_UNDERFLOW).astype(jnp.int32)
                return t + 1, more, new

            _, _, carries = lax.while_loop(cond, step, (jnp.int32(0), jnp.int32(1), carries))
        for g in range(heads):
            o_ref[rows, _head_cols(g)] = carries[g][1].astype(o_ref.dtype)


def _attn_c(proj, batch, seq):
    G = ATT_HEADS
    kern = functools.partial(_attn_c_kernel, scale=HEAD_DIM ** -0.5, heads=G)
    HB = N_HEADS_C // G
    W = G * HEAD_DIM
    return pl.pallas_call(
        kern,
        out_shape=jax.ShapeDtypeStruct((batch * seq, W_C), BF16),
        grid=(batch, HB),
        in_specs=[pl.BlockSpec((seq, W), lambda b, h: (b, h)),
                  pl.BlockSpec((seq, W), lambda b, h: (b, HB + h)),
                  pl.BlockSpec((seq, W), lambda b, h: (b, 2 * HB + h))],
        out_specs=pl.BlockSpec((seq, W), lambda b, h: (b, h)),
        compiler_params=_cparams(("parallel", "parallel")),
        name="attn_stickbreak",
    )(proj, proj, proj)


def _conv4_silu(x, cw, cb):
    row = lax.broadcasted_iota(jnp.int32, x.shape, 0)
    y = cw[3:4, :] * x + cb
    for d in (1, 2, 3):
        xs = jnp.where(row >= d, pltpu.roll(x, d, 0), 0.0)
        y = y + cw[3 - d:4 - d, :] * xs
    return y * jax.nn.sigmoid(y)


def _mlstm_kernel(q_ref, k_ref, v_ref, o_ref, g_ref, cwq_ref, cbq_ref, cwk_ref, cbk_ref,
                  bif_ref, ng_ref, y_ref, qs_ref, ks_ref):
    h = pl.program_id(1)
    S = q_ref.shape[0]
    L = MLSTM_L
    qs_ref[...] = _conv4_silu(q_ref[...].astype(F32), cwq_ref[...], cbq_ref[...]).astype(BF16)
    ks_ref[...] = (_conv4_silu(k_ref[...].astype(F32), cwk_ref[...], cbk_ref[...])
                   * (DQK_D ** -0.5)).astype(BF16)

    row = lax.broadcasted_iota(jnp.int32, (L, L), 0)
    col = lax.broadcasted_iota(jnp.int32, (L, L), 1)
    lower = row >= col
    tril = jnp.where(lower, 1.0, 0.0).astype(BF16)
    lane = lax.broadcasted_iota(jnp.int32, (L, LANE), 1)
    sub = lax.broadcasted_iota(jnp.int32, (LANE, L), 0)
    ng = ng_ref[...]

    def chunk(ci, carry):
        Ct, n, m = carry
        off = pl.multiple_of(ci * L, L)
        gates = g_ref[pl.ds(off, L), :] + bif_ref[...]
        lsig = jnp.minimum(gates, 0.0) - jnp.log(1.0 + jnp.exp(-jnp.abs(gates)))
        li_c = jnp.sum(jnp.where(lane == h, gates, 0.0), axis=-1, keepdims=True)
        lf_c = jnp.sum(jnp.where(lane == h + N_HEADS_D, lsig, 0.0), axis=-1, keepdims=True)
        gates_t = gates.T
        lsig_t = lsig.T
        li_r = jnp.sum(jnp.where(sub == h, gates_t, 0.0), axis=0, keepdims=True)
        lf_r = jnp.sum(jnp.where(sub == h + N_HEADS_D, lsig_t, 0.0), axis=0, keepdims=True)
        fh, fm, fl = _split3(jnp.broadcast_to(lf_c, (L, LANE)))
        b_c = (_dot(tril, fh) + _dot(tril, fm) + _dot(tril, fl))[:, 0:1]
        rh, rm, rl = _split3(jnp.broadcast_to(lf_r, (8, L)))
        b_r = (_dot_nt(rh, tril) + _dot_nt(rm, tril) + _dot_nt(rl, tril))[0:1, :]
        g = jnp.sum(lf_c, axis=0, keepdims=True)

        qc = qs_ref[pl.ds(off, L), :]
        kc = ks_ref[pl.ds(off, L), :]
        vc = v_ref[pl.ds(off, L), :]

        dm = jnp.where(lower, b_c - b_r + li_r, NEG)
        inter = b_c + m
        m_t = jnp.maximum(inter, jnp.max(dm, axis=-1, keepdims=True))
        p = jnp.exp(dm - m_t)
        w_inter = jnp.exp(inter - m_t)
        sqk = _dot_nt(qc, kc) * p
        qf = qc.astype(F32)
        num = w_inter * _dot(qc, Ct.astype(BF16)) + _dot(sqk.astype(BF16), vc)
        den = w_inter * jnp.sum(qf * n, axis=-1, keepdims=True) + jnp.sum(sqk, axis=-1, keepdims=True)
        hh = num / jnp.maximum(jnp.abs(den), jnp.exp(-m_t))

        dec_c = g - b_c + li_c
        m_new = jnp.maximum(g + m, jnp.max(dec_c, axis=0, keepdims=True))
        wk = jnp.exp(dec_c - m_new)
        sc = jnp.exp(g + m - m_new)
        kw = kc.astype(F32) * wk
        Ct_new = sc * Ct + _dot_tn(kw.astype(BF16), vc)
        n_new = sc * n + jnp.sum(kw, axis=0, keepdims=True)

        mu = jnp.mean(hh, axis=-1, keepdims=True)
        hc = hh - mu
        var = jnp.mean(hc * hc, axis=-1, keepdims=True)
        hn = hc * lax.rsqrt(var + LN_EPS) * ng
        og = o_ref[pl.ds(off, L), :].astype(F32)
        y_ref[pl.ds(off, L), :] = (jax.nn.sigmoid(og) * hn).astype(y_ref.dtype)
        return Ct_new, n_new, m_new

    init = (jnp.zeros((DQK_D, DV_D), F32), jnp.zeros((1, DQK_D), F32), jnp.zeros((1, 1), F32))
    lax.fori_loop(0, S // L, chunk, init)


def _mlstm(proj, gates, conv_w, conv_b, b_if, norm_g, batch, seq):
    H = N_HEADS_D
    qk0 = 3 * W_C // DQK_D
    v0 = (3 * W_C + 2 * QK_D) // DV_D
    o0 = v0 + H
    return pl.pallas_call(
        _mlstm_kernel,
        out_shape=jax.ShapeDtypeStruct((batch * seq, W_D), BF16),
        grid=(batch, H),
        in_specs=[pl.BlockSpec((seq, DQK_D), lambda b, h: (b, qk0 + h)),
                  pl.BlockSpec((seq, DQK_D), lambda b, h: (b, qk0 + H + h)),
                  pl.BlockSpec((seq, DV_D), lambda b, h: (b, v0 + h)),
                  pl.BlockSpec((seq, DV_D), lambda b, h: (b, o0 + h)),
                  pl.BlockSpec((seq, GATE_PAD), lambda b, h: (b, 0)),
                  pl.BlockSpec((4, DQK_D), lambda b, h: (0, h)),
                  pl.BlockSpec((1, DQK_D), lambda b, h: (0, h)),
                  pl.BlockSpec((4, DQK_D), lambda b, h: (0, H + h)),
                  pl.BlockSpec((1, DQK_D), lambda b, h: (0, H + h)),
                  pl.BlockSpec((1, GATE_PAD), lambda b, h: (0, 0)),
                  pl.BlockSpec((1, DV_D), lambda b, h: (0, h))],
        out_specs=pl.BlockSpec((seq, DV_D), lambda b, h: (b, h)),
        scratch_shapes=[pltpu.VMEM((seq, DQK_D), BF16), pltpu.VMEM((seq, DQK_D), BF16)],
        compiler_params=_cparams(("parallel", "arbitrary")),
        name="mlstm",
    )(proj, proj, proj, proj, gates, conv_w, conv_b, conv_w, conv_b, b_if, norm_g)


def _conv_ffn_block(x, xb, w_up, w_gate, conv_w, conv_b, w_down, ln_g, ln_b, seq):
    h = _ffn_up(xb, w_up, w_gate, conv_w, conv_b.reshape(1, D_FF), seq)
    z = _ffn_down_resid(h, w_down.astype(BF16), x)
    return _layer_norm(z, ln_g, ln_b)


def kernel(x, w_in_ab, w_out_ab, w_in_cd, b_if_cd, conv_cd, conv_cd_b, norm_cd_g, w_out_cd,
           ffn_w_up, ffn_w_gate, ffn_conv, ffn_conv_b, ffn_w_down, ln_g, ln_b):
    B, S, D = x.shape
    x = x.reshape(B * S, D)
    xb = x.astype(BF16)

    qkv = _matmul(xb, w_in_ab[0], IN_MAIN, BF16)
    ya = _attn_a(qkv, jnp.asarray(_dilated_bias_table()), B, S)
    yb = _attn_b(qkv, jnp.asarray(_moba_mean_matrix(S), BF16), B, S)
    z = _out_proj_resid(ya, yb, w_out_ab[0], x)
    x, xb = _layer_norm(z, ln_g[0, 0], ln_b[0, 0])
    x, xb = _conv_ffn_block(x, xb, ffn_w_up[0], ffn_w_gate[0], ffn_conv[0], ffn_conv_b[0],
                            ffn_w_down[0], ln_g[0, 1], ln_b[0, 1], S)

    w_in = w_in_cd[0]
    proj = _matmul(xb, w_in, IN_MAIN, BF16)
    n_gate = 2 * N_HEADS_D
    w_gate = jnp.pad(w_in[:, IN_MAIN:], ((0, 0), (0, GATE_PAD - n_gate)))
    gates = _matmul(xb, w_gate, GATE_PAD, F32)
    yc = _attn_c(proj, B, S)
    b_if = jnp.pad(b_if_cd[0], ((0, GATE_PAD - n_gate),)).reshape(1, GATE_PAD)
    yd = _mlstm(proj, gates, conv_cd[0], conv_cd_b[0].reshape(1, 2 * QK_D), b_if,
                norm_cd_g[0].reshape(1, W_D), B, S)
    z = _out_proj_resid(yc, yd, w_out_cd[0], x)
    x, xb = _layer_norm(z, ln_g[1, 0], ln_b[1, 0])
    x, xb = _conv_ffn_block(x, xb, ffn_w_up[1], ffn_w_gate[1], ffn_conv[1], ffn_conv_b[1],
                            ffn_w_down[1], ln_g[1, 1], ln_b[1, 1], S)
    return x.reshape(B, S, D)
```

```python
import functools

import numpy as np
import jax
import jax.numpy as jnp
from jax import lax
from jax.experimental import pallas as pl
from jax.experimental.pallas import tpu as pltpu

F32 = jnp.float32
BF16 = jnp.bfloat16

D_MODEL = 4096
DEPTH = 2
HEAD_DIM = 128
N_HEADS_A = 16
N_HEADS_B = 16
N_HEADS_C = 16
N_HEADS_D = 8
DQK_D = 128
DV_D = 256
D_FF = 11008
LN_EPS = 1e-5
ALPHA = (2 * DEPTH) ** 0.25
MOBA_BLOCK = 256
MOBA_TOPK = 3
DILATED_PATTERNS = ((128, 1), (512, 4), (2048, 16))

W_A = N_HEADS_A * HEAD_DIM
W_B = N_HEADS_B * HEAD_DIM
W_C = N_HEADS_C * HEAD_DIM
QK_D = N_HEADS_D * DQK_D
W_D = N_HEADS_D * DV_D
IN_MAIN = 12288
GATE_PAD = 128

LANE = 128
ATT_BLK = 256
ATT_HEADS = 4
MLSTM_L = 256
FFN_TN = 256
FFN_ROWS = 512
NEG = -1e30
SB_UNDERFLOW = 104.0

VMEM_LIMIT = 56 * 1024 * 1024


def _cparams(sem):
    return pltpu.CompilerParams(dimension_semantics=sem, vmem_limit_bytes=VMEM_LIMIT)


def _dot(a, b):
    return jnp.dot(a, b, preferred_element_type=F32)


def _dot_nt(a, b):
    return lax.dot_general(a, b, (((1,), (1,)), ((), ())), preferred_element_type=F32)


def _dot_tn(a, b):
    return lax.dot_general(a, b, (((0,), (0,)), ((), ())), preferred_element_type=F32)


def _split3(x):
    h = x.astype(BF16)
    r = x - h.astype(F32)
    m = r.astype(BF16)
    l = (r - m.astype(F32)).astype(BF16)
    return h, m, l


def _mm_wres_kernel(a_ref, w_ref, o_ref, wb_ref, *, w_transposed):
    @pl.when(pl.program_id(1) == 0)
    def _():
        w = w_ref[...]
        wb_ref[...] = (w.T if w_transposed else w).astype(BF16)

    o_ref[...] = _dot(a_ref[...], wb_ref[...]).astype(o_ref.dtype)


def _matmul(a, w, col0, n_cols, out_dtype, w_transposed=False, tm=1024, tn=512):
    M, K = a.shape
    tn = min(tn, n_cols)
    j0 = col0 // tn
    assert col0 % tn == 0
    if w_transposed:
        w_spec = pl.BlockSpec((None, tn, K), lambda j, i: (0, j0 + j, 0))
    else:
        w_spec = pl.BlockSpec((None, K, tn), lambda j, i: (0, 0, j0 + j))
    return pl.pallas_call(
        functools.partial(_mm_wres_kernel, w_transposed=w_transposed),
        out_shape=jax.ShapeDtypeStruct((M, n_cols), out_dtype),
        grid=(n_cols // tn, M // tm),
        in_specs=[pl.BlockSpec((tm, K), lambda j, i: (i, 0)), w_spec],
        out_specs=pl.BlockSpec((tm, tn), lambda j, i: (i, j)),
        scratch_shapes=[pltpu.VMEM((K, tn), BF16)],
        compiler_params=_cparams(("arbitrary", "arbitrary")),
        name="matmul",
    )(a, w)


def _mm2_resid_kernel(a1_ref, a2_ref, w1_ref, w2_ref, x_ref, o_ref, wb1_ref, wb2_ref):
    @pl.when(pl.program_id(1) == 0)
    def _():
        wb1_ref[...] = w1_ref[...].astype(BF16)
        wb2_ref[...] = w2_ref[...].astype(BF16)

    y = _dot(a1_ref[...], wb1_ref[...]) + _dot(a2_ref[...], wb2_ref[...])
    o_ref[...] = ALPHA * x_ref[...] + y


def _out_proj_resid(a1, a2, w, x, tm=1024, tn=512):
    M, K1 = a1.shape
    K2 = a2.shape[1]
    assert K1 == K2
    N = w.shape[2]
    return pl.pallas_call(
        _mm2_resid_kernel,
        out_shape=jax.ShapeDtypeStruct((M, N), F32),
        grid=(N // tn, M // tm),
        in_specs=[pl.BlockSpec((tm, K1), lambda j, i: (i, 0)),
                  pl.BlockSpec((tm, K2), lambda j, i: (i, 0)),
                  pl.BlockSpec((None, K1, tn), lambda j, i: (0, 0, j)),
                  pl.BlockSpec((None, K2, tn), lambda j, i: (0, 1, j)),
                  pl.BlockSpec((tm, tn), lambda j, i: (i, j))],
        out_specs=pl.BlockSpec((tm, tn), lambda j, i: (i, j)),
        scratch_shapes=[pltpu.VMEM((K1, tn), BF16), pltpu.VMEM((K2, tn), BF16)],
        compiler_params=_cparams(("arbitrary", "arbitrary")),
        name="out_proj_resid",
    )(a1, a2, w, w, x)


def _mm_resid_kernel(a_ref, w_ref, x_ref, o_ref):
    o_ref[...] = ALPHA * x_ref[...] + _dot(a_ref[...], w_ref[...])


def _ffn_down_resid(h, w, x, layer, tm=512, tn=512):
    M, K = h.shape
    N = w.shape[2]
    return pl.pallas_call(
        _mm_resid_kernel,
        out_shape=jax.ShapeDtypeStruct((M, N), F32),
        grid=(M // tm, N // tn),
        in_specs=[pl.BlockSpec((tm, K), lambda i, j: (i, 0)),
                  pl.BlockSpec((None, K, tn), lambda i, j: (layer, 0, j)),
                  pl.BlockSpec((tm, tn), lambda i, j: (i, j))],
        out_specs=pl.BlockSpec((tm, tn), lambda i, j: (i, j)),
        compiler_params=_cparams(("parallel", "arbitrary")),
        name="ffn_down_resid",
    )(h, w, x)


def _ln_kernel(z_ref, g_ref, b_ref, o_ref, ob_ref):
    z = z_ref[...]
    mu = jnp.mean(z, axis=-1, keepdims=True)
    zc = z - mu
    var = jnp.mean(zc * zc, axis=-1, keepdims=True)
    y = zc * lax.rsqrt(var + LN_EPS) * g_ref[...] + b_ref[...]
    o_ref[...] = y
    ob_ref[...] = y.astype(BF16)


def _layer_norm(z, g, b, tr=256):
    M, N = z.shape
    return pl.pallas_call(
        _ln_kernel,
        out_shape=(jax.ShapeDtypeStruct((M, N), F32), jax.ShapeDtypeStruct((M, N), BF16)),
        grid=(M // tr,),
        in_specs=[pl.BlockSpec((tr, N), lambda i: (i, 0)),
                  pl.BlockSpec((1, N), lambda i: (0, 0)),
                  pl.BlockSpec((1, N), lambda i: (0, 0))],
        out_specs=(pl.BlockSpec((tr, N), lambda i: (i, 0)),
                   pl.BlockSpec((tr, N), lambda i: (i, 0))),
        compiler_params=_cparams(("parallel",)),
        name="layer_norm",
    )(z, g.reshape(1, N), b.reshape(1, N))


def _ffn_up_kernel(x_ref, wu_ref, wg_ref, cw_ref, cb_ref, h_ref, wub_ref, wgb_ref):
    @pl.when(pl.program_id(1) == 0)
    def _():
        wub_ref[...] = wu_ref[...].astype(BF16)
        wgb_ref[...] = wg_ref[...].astype(BF16)

    cw = cw_ref[...]
    cb = cb_ref[...]
    R = FFN_ROWS
    row = lax.broadcasted_iota(jnp.int32, (R, cw.shape[1]), 0)
    tail = jnp.zeros((8, cw.shape[1]), F32)
    for c in range(x_ref.shape[0] // R):
        x = x_ref[c * R:(c + 1) * R, :]
        g = _dot(x, wgb_ref[...])
        u = _dot(x, wub_ref[...])
        gm1 = tail[7:8, :]
        gm2 = tail[6:7, :]
        g1 = jnp.where(row == 0, gm1, pltpu.roll(g, 1, 0))
        g2 = jnp.where(row == 0, gm2, jnp.where(row == 1, gm1, pltpu.roll(g, 2, 0)))
        tail = g[R - 8:, :]
        gc = cw[0:1, :] * g2 + cw[1:2, :] * g1 + cw[2:3, :] * g + cb
        h_ref[c * R:(c + 1) * R, :] = (gc * jax.nn.sigmoid(gc) * u).astype(h_ref.dtype)


def _ffn_up(xb, wu, wg, cw, cb, layer, seq, tn=FFN_TN):
    M, K = xb.shape
    N = wu.shape[2]
    once = pl.Buffered(1)
    return pl.pallas_call(
        _ffn_up_kernel,
        out_shape=jax.ShapeDtypeStruct((M, N), BF16),
        grid=(N // tn, M // seq),
        in_specs=[pl.BlockSpec((seq, K), lambda j, i: (i, 0)),
                  pl.BlockSpec((None, K, tn), lambda j, i: (layer, 0, j), pipeline_mode=once),
                  pl.BlockSpec((None, K, tn), lambda j, i: (layer, 0, j), pipeline_mode=once),
                  pl.BlockSpec((None, 3, tn), lambda j, i: (layer, 0, j)),
                  pl.BlockSpec((None, 1, tn), lambda j, i: (layer, 0, j))],
        out_specs=pl.BlockSpec((seq, tn), lambda j, i: (i, j)),
        scratch_shapes=[pltpu.VMEM((K, tn), BF16), pltpu.VMEM((K, tn), BF16)],
        compiler_params=_cparams(("arbitrary", "arbitrary")),
        name="ffn_up",
    )(xb, wu, wg, cw, cb)


def _dilated_bias_table():
    r = np.arange(ATT_BLK)[:, None]
    c = np.arange(ATT_BLK)[None, :]
    tables = []
    for dl in range(4):
        dist = dl * ATT_BLK + r - c
        cnt = np.zeros_like(dist)
        for (w, d) in DILATED_PATTERNS:
            cnt += ((dist >= 0) & (dist % d == 0) & (dist <= w)).astype(dist.dtype)
        with np.errstate(divide="ignore"):
            tables.append(np.where(cnt > 0, np.log(np.maximum(cnt, 1)), NEG))
    return np.stack(tables).astype(np.float32)


def _head_cols(g):
    return slice(g * HEAD_DIM, (g + 1) * HEAD_DIM)


def _softmax_step(q, kb, vb, bias, carry, scale):
    m, l, acc = carry
    s = _dot_nt(q, kb) * scale + bias
    m_new = jnp.maximum(m, jnp.max(s, axis=-1, keepdims=True))
    a = jnp.exp(m - m_new)
    p = jnp.exp(s - m_new)
    l = a * l + jnp.sum(p, axis=-1, keepdims=True)
    acc = a * acc + _dot(p.astype(BF16), vb)
    return m_new, l, acc


def _attn_a_kernel(q_ref, k_ref, v_ref, bias_ref, o_ref, *, scale, heads):
    S = q_ref.shape[0]
    T = ATT_BLK
    for i in range(S // T):
        rows = slice(i * T, (i + 1) * T)
        qs = [q_ref[rows, _head_cols(g)] for g in range(heads)]

        def step(j, bias, carries):
            off = j * T if isinstance(j, int) else pl.multiple_of(j * T, T)
            return tuple(
                _softmax_step(qs[g], k_ref[pl.ds(off, T), _head_cols(g)], v_ref[pl.ds(off, T), _head_cols(g)],
                              bias, carries[g], scale)
                for g in range(heads))

        carries = tuple((jnp.full((T, 1), NEG, F32), jnp.zeros((T, 1), F32), jnp.zeros((T, HEAD_DIM), F32))
                        for _ in range(heads))
        for dl in range(min(3, i + 1)):
            carries = step(i - dl, bias_ref[dl], carries)
        if i >= 3:
            far = bias_ref[3]
            carries = lax.fori_loop(0, i - 2, lambda j, c: step(j, far, c), carries)
        for g in range(heads):
            m, l, acc = carries[g]
            o_ref[rows, _head_cols(g)] = (acc / l).astype(o_ref.dtype)


def _attn_a(qkv, bias, batch, seq):
    G = ATT_HEADS
    kern = functools.partial(_attn_a_kernel, scale=HEAD_DIM ** -0.5, heads=G)
    HB = N_HEADS_A // G
    W = G * HEAD_DIM
    return pl.pallas_call(
        kern,
        out_shape=jax.ShapeDtypeStruct((batch * seq, W_A), BF16),
        grid=(batch, HB),
        in_specs=[pl.BlockSpec((seq, W), lambda b, h: (b, h)),
                  pl.BlockSpec((seq, W), lambda b, h: (b, HB + h)),
                  pl.BlockSpec((seq, W), lambda b, h: (b, 2 * HB + h)),
                  pl.BlockSpec((4, ATT_BLK, ATT_BLK), lambda b, h: (0, 0, 0))],
        out_specs=pl.BlockSpec((seq, W), lambda b, h: (b, h)),
        compiler_params=_cparams(("parallel", "parallel")),
        name="attn_dilated",
    )(qkv, qkv, qkv, bias)


def _moba_mean_matrix(seq):
    m = np.zeros((LANE, seq), np.float32)
    for j in range(seq // MOBA_BLOCK):
        m[j, j * MOBA_BLOCK:(j + 1) * MOBA_BLOCK] = 1.0 / MOBA_BLOCK
    return m


def _moba_select_bias(q, kmean3, i, lane):
    km_h, km_m, km_l = kmean3
    valid = lane < i
    if i > MOBA_TOPK:
        gate = _dot_nt(q, km_h) + _dot_nt(q, km_m) + _dot_nt(q, km_l)
        rank = jnp.zeros(gate.shape, jnp.int32)
        for jp in range(i):
            gj = gate[:, jp:jp + 1]
            beats = (gj > gate) | ((gj == gate) & (jp < lane))
            rank = rank + beats.astype(jnp.int32)
        sel = valid & (rank < MOBA_TOPK)
    else:
        sel = valid
    return jnp.where(sel, 0.0, NEG)


def _attn_b_kernel(q_ref, k_ref, v_ref, mean_ref, o_ref, *, scale, heads):
    S = q_ref.shape[0]
    T = ATT_BLK
    kmeans = [_split3(_dot(mean_ref[...], k_ref[:, _head_cols(g)])) for g in range(heads)]
    lane = lax.broadcasted_iota(jnp.int32, (T, LANE), 1)
    row = lax.broadcasted_iota(jnp.int32, (T, T), 0)
    col = lax.broadcasted_iota(jnp.int32, (T, T), 1)
    causal_bias = jnp.where(col <= row, 0.0, NEG)

    for i in range(S // T):
        rows = slice(i * T, (i + 1) * T)
        qs = [q_ref[rows, _head_cols(g)] for g in range(heads)]
        init = (jnp.full((T, 1), NEG, F32), jnp.zeros((T, 1), F32), jnp.zeros((T, HEAD_DIM), F32))
        carries = tuple(
            _softmax_step(qs[g], k_ref[rows, _head_cols(g)], v_ref[rows, _head_cols(g)], causal_bias, init, scale)
            for g in range(heads))
        if i > 0:
            selbias = [_moba_select_bias(qs[g], kmeans[g], i, lane) for g in range(heads)]

            def step(j, carries):
                off = pl.multiple_of(j * T, T)
                out = []
                for g in range(heads):
                    bj = jnp.sum(jnp.where(lane == j, selbias[g], 0.0), axis=-1, keepdims=True)
                    out.append(_softmax_step(qs[g], k_ref[pl.ds(off, T), _head_cols(g)],
                                             v_ref[pl.ds(off, T), _head_cols(g)], bj, carries[g], scale))
                return tuple(out)

            carries = lax.fori_loop(0, i, step, carries)
        for g in range(heads):
            m, l, acc = carries[g]
            o_ref[rows, _head_cols(g)] = (acc / l).astype(o_ref.dtype)


def _attn_b(qkv, mean_mat, batch, seq):
    G = ATT_HEADS
    kern = functools.partial(_attn_b_kernel, scale=HEAD_DIM ** -0.5, heads=G)
    HB = N_HEADS_B // G
    W = G * HEAD_DIM
    base = 3 * N_HEADS_A // G
    return pl.pallas_call(
        kern,
        out_shape=jax.ShapeDtypeStruct((batch * seq, W_B), BF16),
        grid=(batch, HB),
        in_specs=[pl.BlockSpec((seq, W), lambda b, h: (b, base + h)),
                  pl.BlockSpec((seq, W), lambda b, h: (b, base + HB + h)),
                  pl.BlockSpec((seq, W), lambda b, h: (b, base + 2 * HB + h)),
                  pl.BlockSpec((LANE, seq), lambda b, h: (0, 0))],
        out_specs=pl.BlockSpec((seq, W), lambda b, h: (b, h)),
        compiler_params=_cparams(("parallel", "parallel")),
        name="attn_moba",
    )(qkv, qkv, qkv, mean_mat)


def _attn_c_kernel(q_ref, k_ref, v_ref, o_ref, *, scale, heads):
    S = q_ref.shape[0]
    T = ATT_BLK
    row = lax.broadcasted_iota(jnp.int32, (T, T), 0)
    col = lax.broadcasted_iota(jnp.int32, (T, T), 1)
    tri = jnp.where(row >= col, 1.0, 0.0).astype(BF16)
    strict = col < row

    def block(q, kb, vb, carry, diag):
        c, acc = carry
        z = _dot_nt(q, kb) * scale
        sp = jnp.maximum(z, 0.0) + jnp.log(1.0 + jnp.exp(-jnp.abs(z)))
        if diag:
            sp = jnp.where(strict, sp, 0.0)
        sp_h = sp.astype(BF16)
        sp_l = (sp - sp_h.astype(F32)).astype(BF16)
        rc_in = _dot(sp_h, tri) + _dot(sp_l, tri)
        a = jnp.exp(z - (rc_in + c))
        if diag:
            a = jnp.where(strict, a, 0.0)
        acc = acc + _dot(a.astype(BF16), vb)
        c = c + rc_in[:, 0:1]
        return c, acc

    for i in range(S // T):
        rows = slice(i * T, (i + 1) * T)
        qs = [q_ref[rows, _head_cols(g)] for g in range(heads)]
        init = (jnp.zeros((T, 1), F32), jnp.zeros((T, HEAD_DIM), F32))
        carries = tuple(block(qs[g], k_ref[rows, _head_cols(g)], v_ref[rows, _head_cols(g)], init, True)
                        for g in range(heads))
        if i > 0:
            def cond(state):
                return jnp.logical_and(state[0] < i, state[1] > 0)

            def step(state):
                t, _, carries = state
                off = pl.multiple_of((i - 1 - t) * T, T)
                new = tuple(block(qs[g], k_ref[pl.ds(off, T), _head_cols(g)],
                                  v_ref[pl.ds(off, T), _head_cols(g)], carries[g], False)
                            for g in range(heads))
                cmin = new[0][0]
                for g in range(1, heads):
                    cmin = jnp.minimum(cmin, new[g][0])
                more = (jnp.min(cmin) < SB_UNDERFLOW).astype(jnp.int32)
                return t + 1, more, new

            _, _, carries = lax.while_loop(cond, step, (jnp.int32(0), jnp.int32(1), carries))
        for g in range(heads):
            o_ref[rows, _head_cols(g)] = carries[g][1].astype(o_ref.dtype)


def _attn_c(proj, batch, seq):
    G = ATT_HEADS
    kern = functools.partial(_attn_c_kernel, scale=HEAD_DIM ** -0.5, heads=G)
    HB = N_HEADS_C // G
    W = G * HEAD_DIM
    return pl.pallas_call(
        kern,
        out_shape=jax.ShapeDtypeStruct((batch * seq, W_C), BF16),
        grid=(batch, HB),
        in_specs=[pl.BlockSpec((seq, W), lambda b, h: (b, h)),
                  pl.BlockSpec((seq, W), lambda b, h: (b, HB + h)),
                  pl.BlockSpec((seq, W), lambda b, h: (b, 2 * HB + h))],
        out_specs=pl.BlockSpec((seq, W), lambda b, h: (b, h)),
        compiler_params=_cparams(("parallel", "parallel")),
        name="attn_stickbreak",
    )(proj, proj, proj)


def _conv4_silu(x, cw, cb):
    row = lax.broadcasted_iota(jnp.int32, x.shape, 0)
    y = cw[3:4, :] * x + cb
    for d in (1, 2, 3):
        xs = jnp.where(row >= d, pltpu.roll(x, d, 0), 0.0)
        y = y + cw[3 - d:4 - d, :] * xs
    return y * jax.nn.sigmoid(y)


def _mlstm_kernel(q_ref, k_ref, v_ref, o_ref, g_ref, cwq_ref, cbq_ref, cwk_ref, cbk_ref,
                  bif_ref, ng_ref, y_ref, qs_ref, ks_ref):
    h = pl.program_id(1)
    S = q_ref.shape[0]
    L = MLSTM_L
    qs_ref[...] = _conv4_silu(q_ref[...].astype(F32), cwq_ref[...], cbq_ref[...]).astype(BF16)
    ks_ref[...] = (_conv4_silu(k_ref[...].astype(F32), cwk_ref[...], cbk_ref[...])
                   * (DQK_D ** -0.5)).astype(BF16)

    row = lax.broadcasted_iota(jnp.int32, (L, L), 0)
    col = lax.broadcasted_iota(jnp.int32, (L, L), 1)
    lower = row >= col
    tril = jnp.where(lower, 1.0, 0.0).astype(BF16)
    lane = lax.broadcasted_iota(jnp.int32, (L, LANE), 1)
    sub = lax.broadcasted_iota(jnp.int32, (LANE, L), 0)
    ng = ng_ref[...]

    def chunk(ci, carry):
        Ct, n, m = carry
        off = pl.multiple_of(ci * L, L)
        gates = g_ref[pl.ds(off, L), :] + bif_ref[...]
        lsig = jnp.minimum(gates, 0.0) - jnp.log(1.0 + jnp.exp(-jnp.abs(gates)))
        li_c = jnp.sum(jnp.where(lane == h, gates, 0.0), axis=-1, keepdims=True)
        lf_c = jnp.sum(jnp.where(lane == h + N_HEADS_D, lsig, 0.0), axis=-1, keepdims=True)
        gates_t = gates.T
        lsig_t = lsig.T
        li_r = jnp.sum(jnp.where(sub == h, gates_t, 0.0), axis=0, keepdims=True)
        lf_r = jnp.sum(jnp.where(sub == h + N_HEADS_D, lsig_t, 0.0), axis=0, keepdims=True)
        fh, fm, fl = _split3(jnp.broadcast_to(lf_c, (L, LANE)))
        b_c = (_dot(tril, fh) + _dot(tril, fm) + _dot(tril, fl))[:, 0:1]
        rh, rm, rl = _split3(jnp.broadcast_to(lf_r, (8, L)))
        b_r = (_dot_nt(rh, tril) + _dot_nt(rm, tril) + _dot_nt(rl, tril))[0:1, :]
        g = jnp.sum(lf_c, axis=0, keepdims=True)

        qc = qs_ref[pl.ds(off, L), :]
        kc = ks_ref[pl.ds(off, L), :]
        vc = v_ref[pl.ds(off, L), :]

        dm = jnp.where(lower, b_c - b_r + li_r, NEG)
        inter = b_c + m
        m_t = jnp.maximum(inter, jnp.max(dm, axis=-1, keepdims=True))
        p = jnp.exp(dm - m_t)
        w_inter = jnp.exp(inter - m_t)
        sqk = _dot_nt(qc, kc) * p
        qf = qc.astype(F32)
        num = w_inter * _dot(qc, Ct.astype(BF16)) + _dot(sqk.astype(BF16), vc)
        den = w_inter * jnp.sum(qf * n, axis=-1, keepdims=True) + jnp.sum(sqk, axis=-1, keepdims=True)
        hh = num / jnp.maximum(jnp.abs(den), jnp.exp(-m_t))

        dec_c = g - b_c + li_c
        m_new = jnp.maximum(g + m, jnp.max(dec_c, axis=0, keepdims=True))
        wk = jnp.exp(dec_c - m_new)
        sc = jnp.exp(g + m - m_new)
        kw = kc.astype(F32) * wk
        Ct_new = sc * Ct + _dot_tn(kw.astype(BF16), vc)
        n_new = sc * n + jnp.sum(kw, axis=0, keepdims=True)

        mu = jnp.mean(hh, axis=-1, keepdims=True)
        hc = hh - mu
        var = jnp.mean(hc * hc, axis=-1, keepdims=True)
        hn = hc * lax.rsqrt(var + LN_EPS) * ng
        og = o_ref[pl.ds(off, L), :].astype(F32)
        y_ref[pl.ds(off, L), :] = (jax.nn.sigmoid(og) * hn).astype(y_ref.dtype)
        return Ct_new, n_new, m_new

    init = (jnp.zeros((DQK_D, DV_D), F32), jnp.zeros((1, DQK_D), F32), jnp.zeros((1, 1), F32))
    lax.fori_loop(0, S // L, chunk, init)


def _mlstm(proj, gates, conv_w, conv_b, b_if, norm_g, batch, seq):
    H = N_HEADS_D
    qk0 = 3 * W_C // DQK_D
    v0 = (3 * W_C + 2 * QK_D) // DV_D
    o0 = v0 + H
    return pl.pallas_call(
        _mlstm_kernel,
        out_shape=jax.ShapeDtypeStruct((batch * seq, W_D), BF16),
        grid=(batch, H),
        in_specs=[pl.BlockSpec((seq, DQK_D), lambda b, h: (b, qk0 + h)),
                  pl.BlockSpec((seq, DQK_D), lambda b, h: (b, qk0 + H + h)),
                  pl.BlockSpec((seq, DV_D), lambda b, h: (b, v0 + h)),
                  pl.BlockSpec((seq, DV_D), lambda b, h: (b, o0 + h)),
                  pl.BlockSpec((seq, GATE_PAD), lambda b, h: (b, 0)),
                  pl.BlockSpec((4, DQK_D), lambda b, h: (0, h)),
                  pl.BlockSpec((1, DQK_D), lambda b, h: (0, h)),
                  pl.BlockSpec((4, DQK_D), lambda b, h: (0, H + h)),
                  pl.BlockSpec((1, DQK_D), lambda b, h: (0, H + h)),
                  pl.BlockSpec((1, GATE_PAD), lambda b, h: (0, 0)),
                  pl.BlockSpec((1, DV_D), lambda b, h: (0, h))],
        out_specs=pl.BlockSpec((seq, DV_D), lambda b, h: (b, h)),
        scratch_shapes=[pltpu.VMEM((seq, DQK_D), BF16), pltpu.VMEM((seq, DQK_D), BF16)],
        compiler_params=_cparams(("parallel", "arbitrary")),
        name="mlstm",
    )(proj, proj, proj, proj, gates, conv_w, conv_b, conv_w, conv_b, b_if, norm_g)


def _conv_ffn_block(x, xb, w_up, w_gate, conv_w, conv_b, w_down_bf16, layer, ln_g, ln_b, seq):
    h = _ffn_up(xb, w_up, w_gate, conv_w, conv_b.reshape(DEPTH, 1, D_FF), layer, seq)
    z = _ffn_down_resid(h, w_down_bf16, x, layer)
    return _layer_norm(z, ln_g, ln_b)


def kernel(x, w_in_ab, w_out_ab, w_in_cd, b_if_cd, conv_cd, conv_cd_b, norm_cd_g, w_out_cd,
           ffn_w_up, ffn_w_gate, ffn_conv, ffn_conv_b, ffn_w_down, ln_g, ln_b):
    B, S, D = x.shape
    x = x.reshape(B * S, D)
    xb = x.astype(BF16)
    w_down = ffn_w_down.astype(BF16)
    ffn = (ffn_w_up, ffn_w_gate, ffn_conv, ffn_conv_b, w_down)

    qkv = _matmul(xb, w_in_ab, 0, IN_MAIN, BF16)
    ya = _attn_a(qkv, jnp.asarray(_dilated_bias_table()), B, S)
    yb = _attn_b(qkv, jnp.asarray(_moba_mean_matrix(S), BF16), B, S)
    z = _out_proj_resid(ya, yb, w_out_ab, x)
    x, xb = _layer_norm(z, ln_g[0, 0], ln_b[0, 0])
    x, xb = _conv_ffn_block(x, xb, *ffn, 0, ln_g[0, 1], ln_b[0, 1], S)

    w_in_t = jnp.swapaxes(w_in_cd, 1, 2)
    proj = _matmul(xb, w_in_t, 0, IN_MAIN, BF16, w_transposed=True)
    gates = _matmul(xb, w_in_t, IN_MAIN, GATE_PAD, F32, w_transposed=True)
    yc = _attn_c(proj, B, S)
    b_if = jnp.pad(b_if_cd[0], ((0, GATE_PAD - 2 * N_HEADS_D),)).reshape(1, GATE_PAD)
    yd = _mlstm(proj, gates, conv_cd[0], conv_cd_b[0].reshape(1, 2 * QK_D), b_if,
                norm_cd_g[0].reshape(1, W_D), B, S)
    z = _out_proj_resid(yc, yd, w_out_cd, x)
    x, xb = _layer_norm(z, ln_g[1, 0], ln_b[1, 0])
    x, xb = _conv_ffn_block(x, xb, *ffn, 1, ln_g[1, 1], ln_b[1, 1], S)
    return x.reshape(B, S, D)
```

```python
import functools

import numpy as np
import jax
import jax.numpy as jnp
from jax import lax
from jax.experimental import pallas as pl
from jax.experimental.pallas import tpu as pltpu

F32 = jnp.float32
BF16 = jnp.bfloat16

D_MODEL = 4096
DEPTH = 2
HEAD_DIM = 128
N_HEADS_A = 16
N_HEADS_B = 16
N_HEADS_C = 16
N_HEADS_D = 8
DQK_D = 128
DV_D = 256
D_FF = 11008
LN_EPS = 1e-5
ALPHA = (2 * DEPTH) ** 0.25
MOBA_BLOCK = 256
MOBA_TOPK = 3
MOBA_ROWS = 16
DILATED_PATTERNS = ((128, 1), (512, 4), (2048, 16))

W_A = N_HEADS_A * HEAD_DIM
W_B = N_HEADS_B * HEAD_DIM
W_C = N_HEADS_C * HEAD_DIM
QK_D = N_HEADS_D * DQK_D
W_D = N_HEADS_D * DV_D
IN_MAIN = 12288
GATE_PAD = 128

LANE = 128
ATT_BLK = 256
ATT_HEADS = 8
MLSTM_L = 256
FFN_TN = 256
FFN_ROWS = 512
NEG = -1e30
SB_UNDERFLOW = 104.0

VMEM_LIMIT = 56 * 1024 * 1024


def _cparams(sem):
    return pltpu.CompilerParams(dimension_semantics=sem, vmem_limit_bytes=VMEM_LIMIT)


def _dot(a, b):
    return jnp.dot(a, b, preferred_element_type=F32)


def _dot_nt(a, b):
    return lax.dot_general(a, b, (((1,), (1,)), ((), ())), preferred_element_type=F32)


def _dot_tn(a, b):
    return lax.dot_general(a, b, (((0,), (0,)), ((), ())), preferred_element_type=F32)


def _split3(x):
    h = x.astype(BF16)
    r = x - h.astype(F32)
    m = r.astype(BF16)
    l = (r - m.astype(F32)).astype(BF16)
    return h, m, l


def _mm_wres_kernel(a_ref, w_ref, o_ref, wb_ref, *, w_transposed):
    @pl.when(pl.program_id(1) == 0)
    def _():
        w = w_ref[...]
        wb_ref[...] = (w.T if w_transposed else w).astype(BF16)

    o_ref[...] = _dot(a_ref[...], wb_ref[...]).astype(o_ref.dtype)


def _matmul(a, w, col0, n_cols, out_dtype, w_transposed=False, tm=1024, tn=512):
    M, K = a.shape
    tn = min(tn, n_cols)
    j0 = col0 // tn
    assert col0 % tn == 0
    if w_transposed:
        w_spec = pl.BlockSpec((None, tn, K), lambda j, i: (0, j0 + j, 0))
    else:
        w_spec = pl.BlockSpec((None, K, tn), lambda j, i: (0, 0, j0 + j))
    return pl.pallas_call(
        functools.partial(_mm_wres_kernel, w_transposed=w_transposed),
        out_shape=jax.ShapeDtypeStruct((M, n_cols), out_dtype),
        grid=(n_cols // tn, M // tm),
        in_specs=[pl.BlockSpec((tm, K), lambda j, i: (i, 0)), w_spec],
        out_specs=pl.BlockSpec((tm, tn), lambda j, i: (i, j)),
        scratch_shapes=[pltpu.VMEM((K, tn), BF16)],
        compiler_params=_cparams(("arbitrary", "arbitrary")),
        name="matmul",
    )(a, w)


def _resid_tile(res_refs):
    if len(res_refs) == 1:
        return res_refs[0][...]
    z_ref, mu_ref, rs_ref, g_ref, b_ref = res_refs
    return (z_ref[...] - mu_ref[...]) * rs_ref[...] * g_ref[...] + b_ref[...]


def _resid_specs(res, tm, tn, ij):
    tile = pl.BlockSpec((tm, tn), lambda *g: ij(*g))
    if len(res) == 1:
        return [tile]
    row = pl.BlockSpec((tm, 1), lambda *g: (ij(*g)[0], 0))
    col = pl.BlockSpec((1, tn), lambda *g: (0, ij(*g)[1]))
    return [tile, row, row, col, col]


def _mm2_resid_kernel(a1_ref, a2_ref, w1_ref, w2_ref, *refs):
    *res_refs, o_ref, wb1_ref, wb2_ref = refs

    @pl.when(pl.program_id(1) == 0)
    def _():
        wb1_ref[...] = w1_ref[...].astype(BF16)
        wb2_ref[...] = w2_ref[...].astype(BF16)

    y = _dot(a1_ref[...], wb1_ref[...]) + _dot(a2_ref[...], wb2_ref[...])
    o_ref[...] = ALPHA * _resid_tile(res_refs) + y


def _out_proj_resid(a1, a2, w, res, tm=1024, tn=512):
    M, K1 = a1.shape
    K2 = a2.shape[1]
    assert K1 == K2
    N = w.shape[2]
    return pl.pallas_call(
        _mm2_resid_kernel,
        out_shape=jax.ShapeDtypeStruct((M, N), F32),
        grid=(N // tn, M // tm),
        in_specs=[pl.BlockSpec((tm, K1), lambda j, i: (i, 0)),
                  pl.BlockSpec((tm, K2), lambda j, i: (i, 0)),
                  pl.BlockSpec((None, K1, tn), lambda j, i: (0, 0, j)),
                  pl.BlockSpec((None, K2, tn), lambda j, i: (0, 1, j)),
                  *_resid_specs(res, tm, tn, lambda j, i: (i, j))],
        out_specs=pl.BlockSpec((tm, tn), lambda j, i: (i, j)),
        scratch_shapes=[pltpu.VMEM((K1, tn), BF16), pltpu.VMEM((K2, tn), BF16)],
        compiler_params=_cparams(("arbitrary", "arbitrary")),
        name="out_proj_resid",
    )(a1, a2, w, w, *res)


def _mm_resid_kernel(a_ref, w_ref, *refs):
    *res_refs, o_ref = refs
    o_ref[...] = ALPHA * _resid_tile(res_refs) + _dot(a_ref[...], w_ref[...])


def _ffn_down_resid(h, w, res, layer, tm=512, tn=512):
    M, K = h.shape
    N = w.shape[2]
    return pl.pallas_call(
        _mm_resid_kernel,
        out_shape=jax.ShapeDtypeStruct((M, N), F32),
        grid=(M // tm, N // tn),
        in_specs=[pl.BlockSpec((tm, K), lambda i, j: (i, 0)),
                  pl.BlockSpec((None, K, tn), lambda i, j: (layer, 0, j)),
                  *_resid_specs(res, tm, tn, lambda i, j: (i, j))],
        out_specs=pl.BlockSpec((tm, tn), lambda i, j: (i, j)),
        compiler_params=_cparams(("parallel", "arbitrary")),
        name="ffn_down_resid",
    )(h, w, *res)


def _ln_stats(z):
    mu = jnp.mean(z, axis=-1, keepdims=True)
    zc = z - mu
    var = jnp.mean(zc * zc, axis=-1, keepdims=True)
    return mu, lax.rsqrt(var + LN_EPS)


def _ln_bf16_kernel(z_ref, g_ref, b_ref, ob_ref, mu_ref, rs_ref):
    z = z_ref[...]
    mu, rs = _ln_stats(z)
    ob_ref[...] = ((z - mu) * rs * g_ref[...] + b_ref[...]).astype(BF16)
    mu_ref[...] = mu
    rs_ref[...] = rs


def _ln_f32_kernel(z_ref, g_ref, b_ref, o_ref):
    z = z_ref[...]
    mu, rs = _ln_stats(z)
    o_ref[...] = (z - mu) * rs * g_ref[...] + b_ref[...]


def _layer_norm(z, g, b, final=False, tr=256):
    M, N = z.shape
    g = g.reshape(1, N)
    b = b.reshape(1, N)
    rows = pl.BlockSpec((tr, N), lambda i: (i, 0))
    stat = pl.BlockSpec((tr, 1), lambda i: (i, 0))
    vec = pl.BlockSpec((1, N), lambda i: (0, 0))
    if final:
        return pl.pallas_call(
            _ln_f32_kernel,
            out_shape=jax.ShapeDtypeStruct((M, N), F32),
            grid=(M // tr,), in_specs=[rows, vec, vec], out_specs=rows,
            compiler_params=_cparams(("parallel",)), name="layer_norm_out",
        )(z, g, b)
    xb, mu, rs = pl.pallas_call(
        _ln_bf16_kernel,
        out_shape=(jax.ShapeDtypeStruct((M, N), BF16), jax.ShapeDtypeStruct((M, 1), F32),
                   jax.ShapeDtypeStruct((M, 1), F32)),
        grid=(M // tr,), in_specs=[rows, vec, vec], out_specs=(rows, stat, stat),
        compiler_params=_cparams(("parallel",)), name="layer_norm",
    )(z, g, b)
    return xb, (z, mu, rs, g, b)


def _ffn_up_kernel(x_ref, wu_ref, wg_ref, cw_ref, cb_ref, h_ref, wub_ref, wgb_ref):
    @pl.when(pl.program_id(1) == 0)
    def _():
        wub_ref[...] = wu_ref[...].astype(BF16)
        wgb_ref[...] = wg_ref[...].astype(BF16)

    cw = cw_ref[...]
    cb = cb_ref[...]
    R = FFN_ROWS
    row = lax.broadcasted_iota(jnp.int32, (R, cw.shape[1]), 0)
    tail = jnp.zeros((8, cw.shape[1]), F32)
    for c in range(x_ref.shape[0] // R):
        x = x_ref[c * R:(c + 1) * R, :]
        g = _dot(x, wgb_ref[...])
        u = _dot(x, wub_ref[...])
        gm1 = tail[7:8, :]
        gm2 = tail[6:7, :]
        g1 = jnp.where(row == 0, gm1, pltpu.roll(g, 1, 0))
        g2 = jnp.where(row == 0, gm2, jnp.where(row == 1, gm1, pltpu.roll(g, 2, 0)))
        tail = g[R - 8:, :]
        gc = cw[0:1, :] * g2 + cw[1:2, :] * g1 + cw[2:3, :] * g + cb
        h_ref[c * R:(c + 1) * R, :] = (gc * jax.nn.sigmoid(gc) * u).astype(h_ref.dtype)


def _ffn_up(xb, wu, wg, cw, cb, layer, seq, tn=FFN_TN):
    M, K = xb.shape
    N = wu.shape[2]
    once = pl.Buffered(1)
    return pl.pallas_call(
        _ffn_up_kernel,
        out_shape=jax.ShapeDtypeStruct((M, N), BF16),
        grid=(N // tn, M // seq),
        in_specs=[pl.BlockSpec((seq, K), lambda j, i: (i, 0)),
                  pl.BlockSpec((None, K, tn), lambda j, i: (layer, 0, j), pipeline_mode=once),
                  pl.BlockSpec((None, K, tn), lambda j, i: (layer, 0, j), pipeline_mode=once),
                  pl.BlockSpec((None, 3, tn), lambda j, i: (layer, 0, j)),
                  pl.BlockSpec((None, 1, tn), lambda j, i: (layer, 0, j))],
        out_specs=pl.BlockSpec((seq, tn), lambda j, i: (i, j)),
        scratch_shapes=[pltpu.VMEM((K, tn), BF16), pltpu.VMEM((K, tn), BF16)],
        compiler_params=_cparams(("arbitrary", "arbitrary")),
        name="ffn_up",
    )(xb, wu, wg, cw, cb)


def _dilated_bias_table():
    r = np.arange(ATT_BLK)[None, :]
    c = np.arange(ATT_BLK)[:, None]
    tables = []
    for dl in range(4):
        dist = dl * ATT_BLK + r - c
        cnt = np.zeros_like(dist)
        for (w, d) in DILATED_PATTERNS:
            cnt += ((dist >= 0) & (dist % d == 0) & (dist <= w)).astype(dist.dtype)
        with np.errstate(divide="ignore"):
            tables.append(np.where(cnt > 0, np.log(np.maximum(cnt, 1)), NEG))
    return np.stack(tables).astype(np.float32)


def _head_cols(g):
    return slice(g * HEAD_DIM, (g + 1) * HEAD_DIM)


def _softmax_init():
    T = ATT_BLK
    return jnp.full((1, T), NEG, F32), jnp.zeros((1, T), F32), jnp.zeros((HEAD_DIM, T), F32)


def _softmax_steps(qs, kbs, vbs, biases, carries, scale):
    n = len(qs)
    ss = [_dot_nt(kbs[g], qs[g]) * scale + biases[g] for g in range(n)]
    mid = []
    for g in range(n):
        m, l, acc = carries[g]
        m_new = jnp.maximum(m, jnp.max(ss[g], axis=0, keepdims=True))
        a = jnp.exp(m - m_new)
        p = jnp.exp(ss[g] - m_new)
        l = a * l + jnp.sum(p, axis=0, keepdims=True)
        mid.append((m_new, l, a, p.astype(BF16)))
    out = []
    for g in range(n):
        m_new, l, a, p = mid[g]
        out.append((m_new, l, a * carries[g][2] + _dot_tn(vbs[g], p)))
    return tuple(out)


def _softmax_finish(carry, dtype):
    m, l, acc = carry
    return (acc / l).T.astype(dtype)


def _attn_a_kernel(q_ref, k_ref, v_ref, bias_ref, o_ref, *, scale, heads):
    S = q_ref.shape[0]
    T = ATT_BLK
    for i in range(S // T):
        rows = slice(i * T, (i + 1) * T)
        qs = [q_ref[rows, _head_cols(g)] for g in range(heads)]

        def step(j, bias, carries):
            off = j * T if isinstance(j, int) else pl.multiple_of(j * T, T)
            return _softmax_steps(qs, [k_ref[pl.ds(off, T), _head_cols(g)] for g in range(heads)],
                                  [v_ref[pl.ds(off, T), _head_cols(g)] for g in range(heads)],
                                  [bias] * heads, carries, scale)

        carries = tuple(_softmax_init() for _ in range(heads))
        for dl in range(min(3, i + 1)):
            carries = step(i - dl, bias_ref[dl], carries)
        if i >= 3:
            far = bias_ref[3]
            carries = lax.fori_loop(0, i - 2, lambda j, c: step(j, far, c), carries)
        for g in range(heads):
            o_ref[rows, _head_cols(g)] = _softmax_finish(carries[g], o_ref.dtype)


def _attn_a(qkv, bias, batch, seq):
    G = ATT_HEADS
    kern = functools.partial(_attn_a_kernel, scale=HEAD_DIM ** -0.5, heads=G)
    HB = N_HEADS_A // G
    W = G * HEAD_DIM
    return pl.pallas_call(
        kern,
        out_shape=jax.ShapeDtypeStruct((batch * seq, W_A), BF16),
        grid=(batch, HB),
        in_specs=[pl.BlockSpec((seq, W), lambda b, h: (b, h)),
                  pl.BlockSpec((seq, W), lambda b, h: (b, HB + h)),
                  pl.BlockSpec((seq, W), lambda b, h: (b, 2 * HB + h)),
                  pl.BlockSpec((4, ATT_BLK, ATT_BLK), lambda b, h: (0, 0, 0))],
        out_specs=pl.BlockSpec((seq, W), lambda b, h: (b, h)),
        compiler_params=_cparams(("parallel", "parallel")),
        name="attn_dilated",
    )(qkv, qkv, qkv, bias)


def _moba_mean_matrix(seq):
    assert seq // MOBA_BLOCK <= MOBA_ROWS
    m = np.zeros((MOBA_ROWS, seq), np.float32)
    for j in range(seq // MOBA_BLOCK):
        m[j, j * MOBA_BLOCK:(j + 1) * MOBA_BLOCK] = 1.0 / MOBA_BLOCK
    return m


def _moba_select_bias(q, kmean3, i, blk):
    km_h, km_m, km_l = kmean3
    valid = blk < i
    if i > MOBA_TOPK:
        gate = _dot_nt(km_h, q) + _dot_nt(km_m, q) + _dot_nt(km_l, q)
        rank = jnp.zeros(gate.shape, jnp.int32)
        for jp in range(i):
            gj = gate[jp:jp + 1, :]
            beats = (gj > gate) | ((gj == gate) & (jp < blk))
            rank = rank + beats.astype(jnp.int32)
        sel = valid & (rank < MOBA_TOPK)
    else:
        sel = valid
    return jnp.where(sel, 0.0, NEG)


def _attn_b_kernel(q_ref, k_ref, v_ref, mean_ref, o_ref, *, scale, heads):
    S = q_ref.shape[0]
    T = ATT_BLK
    kmeans = [_split3(_dot(mean_ref[...], k_ref[:, _head_cols(g)])) for g in range(heads)]
    blk = lax.broadcasted_iota(jnp.int32, (MOBA_ROWS, T), 0)
    key = lax.broadcasted_iota(jnp.int32, (T, T), 0)
    qry = lax.broadcasted_iota(jnp.int32, (T, T), 1)
    causal_bias = jnp.where(key <= qry, 0.0, NEG)

    for i in range(S // T):
        rows = slice(i * T, (i + 1) * T)
        qs = [q_ref[rows, _head_cols(g)] for g in range(heads)]
        carries = _softmax_steps(qs, [k_ref[rows, _head_cols(g)] for g in range(heads)],
                                 [v_ref[rows, _head_cols(g)] for g in range(heads)],
                                 [causal_bias] * heads, [_softmax_init()] * heads, scale)
        if i > 0:
            selbias = [_moba_select_bias(qs[g], kmeans[g], i, blk) for g in range(heads)]

            def step(j, carries):
                off = pl.multiple_of(j * T, T)
                bjs = [jnp.sum(jnp.where(blk == j, selbias[g], 0.0), axis=0, keepdims=True)
                       for g in range(heads)]
                return _softmax_steps(qs, [k_ref[pl.ds(off, T), _head_cols(g)] for g in range(heads)],
                                      [v_ref[pl.ds(off, T), _head_cols(g)] for g in range(heads)],
                                      bjs, carries, scale)

            carries = lax.fori_loop(0, i, step, carries)
        for g in range(heads):
            o_ref[rows, _head_cols(g)] = _softmax_finish(carries[g], o_ref.dtype)


def _attn_b(qkv, mean_mat, batch, seq):
    G = ATT_HEADS
    kern = functools.partial(_attn_b_kernel, scale=HEAD_DIM ** -0.5, heads=G)
    HB = N_HEADS_B // G
    W = G * HEAD_DIM
    base = 3 * N_HEADS_A // G
    return pl.pallas_call(
        kern,
        out_shape=jax.ShapeDtypeStruct((batch * seq, W_B), BF16),
        grid=(batch, HB),
        in_specs=[pl.BlockSpec((seq, W), lambda b, h: (b, base + h)),
                  pl.BlockSpec((seq, W), lambda b, h: (b, base + HB + h)),
                  pl.BlockSpec((seq, W), lambda b, h: (b, base + 2 * HB + h)),
                  pl.BlockSpec((MOBA_ROWS, seq), lambda b, h: (0, 0))],
        out_specs=pl.BlockSpec((seq, W), lambda b, h: (b, h)),
        compiler_params=_cparams(("parallel", "parallel")),
        name="attn_moba",
    )(qkv, qkv, qkv, mean_mat)


def _attn_c_kernel(q_ref, k_ref, v_ref, o_ref, *, scale, heads):
    S = q_ref.shape[0]
    T = ATT_BLK
    key = lax.broadcasted_iota(jnp.int32, (T, T), 0)
    qry = lax.broadcasted_iota(jnp.int32, (T, T), 1)
    tri = jnp.where(qry >= key, 1.0, 0.0).astype(BF16)
    strict = key < qry

    def blocks(qs, row_sl, carries, diag):
        hs = range(heads)
        zs = [_dot_nt(k_ref[row_sl, _head_cols(g)], qs[g]) * scale for g in hs]
        sps = []
        for g in hs:
            sp = jnp.maximum(zs[g], 0.0) + jnp.log(1.0 + jnp.exp(-jnp.abs(zs[g])))
            if diag:
                sp = jnp.where(strict, sp, 0.0)
            sp_h = sp.astype(BF16)
            sps.append((sp_h, (sp - sp_h.astype(F32)).astype(BF16)))
        rcs = [_dot(tri, sps[g][0]) + _dot(tri, sps[g][1]) for g in hs]
        ws = []
        for g in hs:
            a = jnp.exp(zs[g] - (rcs[g] + carries[g][0]))
            if diag:
                a = jnp.where(strict, a, 0.0)
            ws.append(a.astype(BF16))
        return tuple((carries[g][0] + rcs[g][0:1, :],
                      carries[g][1] + _dot_tn(v_ref[row_sl, _head_cols(g)], ws[g])) for g in hs)

    for i in range(S // T):
        rows = slice(i * T, (i + 1) * T)
        qs = [q_ref[rows, _head_cols(g)] for g in range(heads)]
        init = (jnp.zeros((1, T), F32), jnp.zeros((HEAD_DIM, T), F32))
        carries = blocks(qs, rows, [init] * heads, True)
        if i > 0:
            def cond(state):
                return jnp.logical_and(state[0] < i, state[1] > 0)

            def step(state):
                t, _, carries = state
                off = pl.multiple_of((i - 1 - t) * T, T)
                new = blocks(qs, pl.ds(off, T), carries, False)
                cmin = new[0][0]
                for g in range(1, heads):
                    cmin = jnp.minimum(cmin, new[g][0])
                more = (jnp.min(cmin) < SB_UNDERFLOW).astype(jnp.int32)
                return t + 1, more, new

            _, _, carries = lax.while_loop(cond, step, (jnp.int32(0), jnp.int32(1), carries))
        for g in range(heads):
            o_ref[rows, _head_cols(g)] = carries[g][1].T.astype(o_ref.dtype)


def _attn_c(proj, batch, seq):
    G = ATT_HEADS
    kern = functools.partial(_attn_c_kernel, scale=HEAD_DIM ** -0.5, heads=G)
    HB = N_HEADS_C // G
    W = G * HEAD_DIM
    return pl.pallas_call(
        kern,
        out_shape=jax.ShapeDtypeStruct((batch * seq, W_C), BF16),
        grid=(batch, HB),
        in_specs=[pl.BlockSpec((seq, W), lambda b, h: (b, h)),
                  pl.BlockSpec((seq, W), lambda b, h: (b, HB + h)),
                  pl.BlockSpec((seq, W), lambda b, h: (b, 2 * HB + h))],
        out_specs=pl.BlockSpec((seq, W), lambda b, h: (b, h)),
        compiler_params=_cparams(("parallel", "parallel")),
        name="attn_stickbreak",
    )(proj, proj, proj)


def _conv4_silu(x, cw, cb):
    row = lax.broadcasted_iota(jnp.int32, x.shape, 0)
    y = cw[3:4, :] * x + cb
    for d in (1, 2, 3):
        xs = jnp.where(row >= d, pltpu.roll(x, d, 0), 0.0)
        y = y + cw[3 - d:4 - d, :] * xs
    return y * jax.nn.sigmoid(y)


def _mlstm_kernel(q_ref, k_ref, v_ref, o_ref, g_ref, cwq_ref, cbq_ref, cwk_ref, cbk_ref,
                  bif_ref, ng_ref, y_ref, qs_ref, ks_ref):
    h = pl.program_id(1)
    S = q_ref.shape[0]
    L = MLSTM_L
    qs_ref[...] = _conv4_silu(q_ref[...].astype(F32), cwq_ref[...], cbq_ref[...]).astype(BF16)
    ks_ref[...] = (_conv4_silu(k_ref[...].astype(F32), cwk_ref[...], cbk_ref[...])
                   * (DQK_D ** -0.5)).astype(BF16)

    row = lax.broadcasted_iota(jnp.int32, (L, L), 0)
    col = lax.broadcasted_iota(jnp.int32, (L, L), 1)
    lower = row >= col
    tril = jnp.where(lower, 1.0, 0.0).astype(BF16)
    lane = lax.broadcasted_iota(jnp.int32, (L, LANE), 1)
    sub = lax.broadcasted_iota(jnp.int32, (LANE, L), 0)
    ng = ng_ref[...]

    def chunk(ci, carry):
        Ct, n, m = carry
        off = pl.multiple_of(ci * L, L)
        gates = g_ref[pl.ds(off, L), :] + bif_ref[...]
        lsig = jnp.minimum(gates, 0.0) - jnp.log(1.0 + jnp.exp(-jnp.abs(gates)))
        li_c = jnp.sum(jnp.where(lane == h, gates, 0.0), axis=-1, keepdims=True)
        lf_c = jnp.sum(jnp.where(lane == h + N_HEADS_D, lsig, 0.0), axis=-1, keepdims=True)
        gates_t = gates.T
        lsig_t = lsig.T
        li_r = jnp.sum(jnp.where(sub == h, gates_t, 0.0), axis=0, keepdims=True)
        lf_r = jnp.sum(jnp.where(sub == h + N_HEADS_D, lsig_t, 0.0), axis=0, keepdims=True)
        fh, fm, fl = _split3(jnp.broadcast_to(lf_c, (L, LANE)))
        b_c = (_dot(tril, fh) + _dot(tril, fm) + _dot(tril, fl))[:, 0:1]
        rh, rm, rl = _split3(jnp.broadcast_to(lf_r, (8, L)))
        b_r = (_dot_nt(rh, tril) + _dot_nt(rm, tril) + _dot_nt(rl, tril))[0:1, :]
        g = jnp.sum(lf_c, axis=0, keepdims=True)

        qc = qs_ref[pl.ds(off, L), :]
        kc = ks_ref[pl.ds(off, L), :]
        vc = v_ref[pl.ds(off, L), :]

        dm = jnp.where(lower, b_c - b_r + li_r, NEG)
        inter = b_c + m
        m_t = jnp.maximum(inter, jnp.max(dm, axis=-1, keepdims=True))
        p = jnp.exp(dm - m_t)
        w_inter = jnp.exp(inter - m_t)
        sqk = _dot_nt(qc, kc) * p
        qf = qc.astype(F32)
        num = w_inter * _dot(qc, Ct.astype(BF16)) + _dot(sqk.astype(BF16), vc)
        den = w_inter * jnp.sum(qf * n, axis=-1, keepdims=True) + jnp.sum(sqk, axis=-1, keepdims=True)
        hh = num / jnp.maximum(jnp.abs(den), jnp.exp(-m_t))

        dec_c = g - b_c + li_c
        m_new = jnp.maximum(g + m, jnp.max(dec_c, axis=0, keepdims=True))
        wk = jnp.exp(dec_c - m_new)
        sc = jnp.exp(g + m - m_new)
        kw = kc.astype(F32) * wk
        Ct_new = sc * Ct + _dot_tn(kw.astype(BF16), vc)
        n_new = sc * n + jnp.sum(kw, axis=0, keepdims=True)

        mu = jnp.mean(hh, axis=-1, keepdims=True)
        hc = hh - mu
        var = jnp.mean(hc * hc, axis=-1, keepdims=True)
        hn = hc * lax.rsqrt(var + LN_EPS) * ng
        og = o_ref[pl.ds(off, L), :].astype(F32)
        y_ref[pl.ds(off, L), :] = (jax.nn.sigmoid(og) * hn).astype(y_ref.dtype)
        return Ct_new, n_new, m_new

    init = (jnp.zeros((DQK_D, DV_D), F32), jnp.zeros((1, DQK_D), F32), jnp.zeros((1, 1), F32))
    lax.fori_loop(0, S // L, chunk, init)


def _mlstm(proj, gates, conv_w, conv_b, b_if, norm_g, batch, seq):
    H = N_HEADS_D
    qk0 = 3 * W_C // DQK_D
    v0 = (3 * W_C + 2 * QK_D) // DV_D
    o0 = v0 + H
    return pl.pallas_call(
        _mlstm_kernel,
        out_shape=jax.ShapeDtypeStruct((batch * seq, W_D), BF16),
        grid=(batch, H),
        in_specs=[pl.BlockSpec((seq, DQK_D), lambda b, h: (b, qk0 + h)),
                  pl.BlockSpec((seq, DQK_D), lambda b, h: (b, qk0 + H + h)),
                  pl.BlockSpec((seq, DV_D), lambda b, h: (b, v0 + h)),
                  pl.BlockSpec((seq, DV_D), lambda b, h: (b, o0 + h)),
                  pl.BlockSpec((seq, GATE_PAD), lambda b, h: (b, 0)),
                  pl.BlockSpec((4, DQK_D), lambda b, h: (0, h)),
                  pl.BlockSpec((1, DQK_D), lambda b, h: (0, h)),
                  pl.BlockSpec((4, DQK_D), lambda b, h: (0, H + h)),
                  pl.BlockSpec((1, DQK_D), lambda b, h: (0, H + h)),
                  pl.BlockSpec((1, GATE_PAD), lambda b, h: (0, 0)),
                  pl.BlockSpec((1, DV_D), lambda b, h: (0, h))],
        out_specs=pl.BlockSpec((seq, DV_D), lambda b, h: (b, h)),
        scratch_shapes=[pltpu.VMEM((seq, DQK_D), BF16), pltpu.VMEM((seq, DQK_D), BF16)],
        compiler_params=_cparams(("parallel", "arbitrary")),
        name="mlstm",
    )(proj, proj, proj, proj, gates, conv_w, conv_b, conv_w, conv_b, b_if, norm_g)


def _conv_ffn_block(res, xb, w_up, w_gate, conv_w, conv_b, w_down_bf16, layer, ln_g, ln_b, seq, final):
    h = _ffn_up(xb, w_up, w_gate, conv_w, conv_b.reshape(DEPTH, 1, D_FF), layer, seq)
    z = _ffn_down_resid(h, w_down_bf16, res, layer)
    return _layer_norm(z, ln_g, ln_b, final=final)


def kernel(x, w_in_ab, w_out_ab, w_in_cd, b_if_cd, conv_cd, conv_cd_b, norm_cd_g, w_out_cd,
           ffn_w_up, ffn_w_gate, ffn_conv, ffn_conv_b, ffn_w_down, ln_g, ln_b):
    B, S, D = x.shape
    x = x.reshape(B * S, D)
    xb = x.astype(BF16)
    w_down = ffn_w_down.astype(BF16)
    ffn = (ffn_w_up, ffn_w_gate, ffn_conv, ffn_conv_b, w_down)

    qkv = _matmul(xb, w_in_ab, 0, IN_MAIN, BF16)
    ya = _attn_a(qkv, jnp.asarray(_dilated_bias_table()), B, S)
    yb = _attn_b(qkv, jnp.asarray(_moba_mean_matrix(S), BF16), B, S)
    z = _out_proj_resid(ya, yb, w_out_ab, (x,))
    xb, res = _layer_norm(z, ln_g[0, 0], ln_b[0, 0])
    xb, res = _conv_ffn_block(res, xb, *ffn, 0, ln_g[0, 1], ln_b[0, 1], S, False)

    w_in_t = jnp.swapaxes(w_in_cd, 1, 2)
    proj = _matmul(xb, w_in_t, 0, IN_MAIN, BF16, w_transposed=True)
    gates = _matmul(xb, w_in_t, IN_MAIN, GATE_PAD, F32, w_transposed=True)
    yc = _attn_c(proj, B, S)
    b_if = jnp.pad(b_if_cd[0], ((0, GATE_PAD - 2 * N_HEADS_D),)).reshape(1, GATE_PAD)
    yd = _mlstm(proj, gates, conv_cd[0], conv_cd_b[0].reshape(1, 2 * QK_D), b_if,
                norm_cd_g[0].reshape(1, W_D), B, S)
    z = _out_proj_resid(yc, yd, w_out_cd, res)
    xb, res = _layer_norm(z, ln_g[1, 0], ln_b[1, 0])
    out = _conv_ffn_block(res, xb, *ffn, 1, ln_g[1, 1], ln_b[1, 1], S, True)
    return out.reshape(B, S, D)
```

```python
import functools

import numpy as np
import jax
import jax.numpy as jnp
from jax import lax
from jax.experimental import pallas as pl
from jax.experimental.pallas import tpu as pltpu

F32 = jnp.float32
BF16 = jnp.bfloat16

D_MODEL = 4096
DEPTH = 2
HEAD_DIM = 128
N_HEADS_A = 16
N_HEADS_B = 16
N_HEADS_C = 16
N_HEADS_D = 8
DQK_D = 128
DV_D = 256
D_FF = 11008
LN_EPS = 1e-5
ALPHA = (2 * DEPTH) ** 0.25
MOBA_BLOCK = 256
MOBA_TOPK = 3
MOBA_ROWS = 16
DILATED_PATTERNS = ((128, 1), (512, 4), (2048, 16))

W_A = N_HEADS_A * HEAD_DIM
W_B = N_HEADS_B * HEAD_DIM
W_C = N_HEADS_C * HEAD_DIM
QK_D = N_HEADS_D * DQK_D
W_D = N_HEADS_D * DV_D
IN_MAIN = 12288
GATE_PAD = 128

LANE = 128
ATT_BLK = 256
ATT_HEADS = 8
MLSTM_L = 256
FFN_TN = 256
FFN_ROWS = 512
NEG = -1e30
SB_UNDERFLOW = 104.0

VMEM_LIMIT = 56 * 1024 * 1024


def _cparams(sem):
    return pltpu.CompilerParams(dimension_semantics=sem, vmem_limit_bytes=VMEM_LIMIT)


def _dot(a, b):
    return jnp.dot(a, b, preferred_element_type=F32)


def _dot_nt(a, b):
    return lax.dot_general(a, b, (((1,), (1,)), ((), ())), preferred_element_type=F32)


def _dot_tn(a, b):
    return lax.dot_general(a, b, (((0,), (0,)), ((), ())), preferred_element_type=F32)


def _split3(x):
    h = x.astype(BF16)
    r = x - h.astype(F32)
    m = r.astype(BF16)
    l = (r - m.astype(F32)).astype(BF16)
    return h, m, l


def _mm_wres_kernel(a_ref, w_ref, o_ref, wb_ref, *, w_transposed):
    @pl.when(pl.program_id(1) == 0)
    def _():
        w = w_ref[...]
        wb_ref[...] = (w.T if w_transposed else w).astype(BF16)

    o_ref[...] = _dot(a_ref[...], wb_ref[...]).astype(o_ref.dtype)


def _matmul(a, w, col0, n_cols, out_dtype, w_transposed=False, tm=1024, tn=512):
    M, K = a.shape
    tn = min(tn, n_cols)
    j0 = col0 // tn
    assert col0 % tn == 0
    if w_transposed:
        w_spec = pl.BlockSpec((None, tn, K), lambda j, i: (0, j0 + j, 0))
    else:
        w_spec = pl.BlockSpec((None, K, tn), lambda j, i: (0, 0, j0 + j))
    return pl.pallas_call(
        functools.partial(_mm_wres_kernel, w_transposed=w_transposed),
        out_shape=jax.ShapeDtypeStruct((M, n_cols), out_dtype),
        grid=(n_cols // tn, M // tm),
        in_specs=[pl.BlockSpec((tm, K), lambda j, i: (i, 0)), w_spec],
        out_specs=pl.BlockSpec((tm, tn), lambda j, i: (i, j)),
        scratch_shapes=[pltpu.VMEM((K, tn), BF16)],
        compiler_params=_cparams(("arbitrary", "arbitrary")),
        name="matmul",
    )(a, w)


def _resid_tile(res_refs):
    if len(res_refs) == 1:
        return res_refs[0][...]
    z_ref, mu_ref, rs_ref, g_ref, b_ref = res_refs
    return (z_ref[...] - mu_ref[...]) * rs_ref[...] * g_ref[...] + b_ref[...]


def _resid_specs(res, tm, tn, ij):
    tile = pl.BlockSpec((tm, tn), lambda *g: ij(*g))
    if len(res) == 1:
        return [tile]
    row = pl.BlockSpec((tm, 1), lambda *g: (ij(*g)[0], 0))
    col = pl.BlockSpec((1, tn), lambda *g: (0, ij(*g)[1]))
    return [tile, row, row, col, col]


def _mm2_resid_kernel(a1_ref, a2_ref, w1_ref, w2_ref, *refs):
    *res_refs, o_ref, wb1_ref, wb2_ref = refs

    @pl.when(pl.program_id(1) == 0)
    def _():
        wb1_ref[...] = w1_ref[...].astype(BF16)
        wb2_ref[...] = w2_ref[...].astype(BF16)

    y = _dot(a1_ref[...], wb1_ref[...]) + _dot(a2_ref[...], wb2_ref[...])
    o_ref[...] = ALPHA * _resid_tile(res_refs) + y


def _out_proj_resid(a1, a2, w, res, tm=1024, tn=512):
    M, K1 = a1.shape
    K2 = a2.shape[1]
    assert K1 == K2
    N = w.shape[2]
    return pl.pallas_call(
        _mm2_resid_kernel,
        out_shape=jax.ShapeDtypeStruct((M, N), F32),
        grid=(N // tn, M // tm),
        in_specs=[pl.BlockSpec((tm, K1), lambda j, i: (i, 0)),
                  pl.BlockSpec((tm, K2), lambda j, i: (i, 0)),
                  pl.BlockSpec((None, K1, tn), lambda j, i: (0, 0, j)),
                  pl.BlockSpec((None, K2, tn), lambda j, i: (0, 1, j)),
                  *_resid_specs(res, tm, tn, lambda j, i: (i, j))],
        out_specs=pl.BlockSpec((tm, tn), lambda j, i: (i, j)),
        scratch_shapes=[pltpu.VMEM((K1, tn), BF16), pltpu.VMEM((K2, tn), BF16)],
        compiler_params=_cparams(("arbitrary", "arbitrary")),
        name="out_proj_resid",
    )(a1, a2, w, w, *res)


def _mm_resid_kernel(a_ref, w_ref, *refs):
    *res_refs, o_ref = refs
    o_ref[...] = ALPHA * _resid_tile(res_refs) + _dot(a_ref[...], w_ref[...])


def _ffn_down_resid(h, w, res, layer, tm=512, tn=512):
    M, K = h.shape
    N = w.shape[2]
    return pl.pallas_call(
        _mm_resid_kernel,
        out_shape=jax.ShapeDtypeStruct((M, N), F32),
        grid=(M // tm, N // tn),
        in_specs=[pl.BlockSpec((tm, K), lambda i, j: (i, 0)),
                  pl.BlockSpec((None, K, tn), lambda i, j: (layer, 0, j)),
                  *_resid_specs(res, tm, tn, lambda i, j: (i, j))],
        out_specs=pl.BlockSpec((tm, tn), lambda i, j: (i, j)),
        compiler_params=_cparams(("parallel", "arbitrary")),
        name="ffn_down_resid",
    )(h, w, *res)


def _ln_stats(z):
    mu = jnp.mean(z, axis=-1, keepdims=True)
    zc = z - mu
    var = jnp.mean(zc * zc, axis=-1, keepdims=True)
    return mu, lax.rsqrt(var + LN_EPS)


def _ln_bf16_kernel(z_ref, g_ref, b_ref, ob_ref, mu_ref, rs_ref):
    z = z_ref[...]
    mu, rs = _ln_stats(z)
    ob_ref[...] = ((z - mu) * rs * g_ref[...] + b_ref[...]).astype(BF16)
    mu_ref[...] = mu
    rs_ref[...] = rs


def _ln_f32_kernel(z_ref, g_ref, b_ref, o_ref):
    z = z_ref[...]
    mu, rs = _ln_stats(z)
    o_ref[...] = (z - mu) * rs * g_ref[...] + b_ref[...]


def _layer_norm(z, g, b, final=False, tr=256):
    M, N = z.shape
    g = g.reshape(1, N)
    b = b.reshape(1, N)
    rows = pl.BlockSpec((tr, N), lambda i: (i, 0))
    stat = pl.BlockSpec((tr, 1), lambda i: (i, 0))
    vec = pl.BlockSpec((1, N), lambda i: (0, 0))
    if final:
        return pl.pallas_call(
            _ln_f32_kernel,
            out_shape=jax.ShapeDtypeStruct((M, N), F32),
            grid=(M // tr,), in_specs=[rows, vec, vec], out_specs=rows,
            compiler_params=_cparams(("parallel",)), name="layer_norm_out",
        )(z, g, b)
    xb, mu, rs = pl.pallas_call(
        _ln_bf16_kernel,
        out_shape=(jax.ShapeDtypeStruct((M, N), BF16), jax.ShapeDtypeStruct((M, 1), F32),
                   jax.ShapeDtypeStruct((M, 1), F32)),
        grid=(M // tr,), in_specs=[rows, vec, vec], out_specs=(rows, stat, stat),
        compiler_params=_cparams(("parallel",)), name="layer_norm",
    )(z, g, b)
    return xb, (z, mu, rs, g, b)


def _ffn_up_kernel(x_ref, wu_ref, wg_ref, cw_ref, cb_ref, h_ref, wub_ref, wgb_ref):
    @pl.when(pl.program_id(1) == 0)
    def _():
        wub_ref[...] = wu_ref[...].astype(BF16)
        wgb_ref[...] = wg_ref[...].astype(BF16)

    cw = cw_ref[...]
    cb = cb_ref[...]
    R = FFN_ROWS
    row = lax.broadcasted_iota(jnp.int32, (R, cw.shape[1]), 0)
    tail = jnp.zeros((8, cw.shape[1]), F32)
    for c in range(x_ref.shape[0] // R):
        x = x_ref[c * R:(c + 1) * R, :]
        g = _dot(x, wgb_ref[...])
        u = _dot(x, wub_ref[...])
        gm1 = tail[7:8, :]
        gm2 = tail[6:7, :]
        g1 = jnp.where(row == 0, gm1, pltpu.roll(g, 1, 0))
        g2 = jnp.where(row == 0, gm2, jnp.where(row == 1, gm1, pltpu.roll(g, 2, 0)))
        tail = g[R - 8:, :]
        gc = cw[0:1, :] * g2 + cw[1:2, :] * g1 + cw[2:3, :] * g + cb
        h_ref[c * R:(c + 1) * R, :] = (gc * jax.nn.sigmoid(gc) * u).astype(h_ref.dtype)


def _ffn_up(xb, wu, wg, cw, cb, layer, seq, tn=FFN_TN):
    M, K = xb.shape
    N = wu.shape[2]
    return pl.pallas_call(
        _ffn_up_kernel,
        out_shape=jax.ShapeDtypeStruct((M, N), BF16),
        grid=(N // tn, M // seq),
        in_specs=[pl.BlockSpec((seq, K), lambda j, i: (i, 0)),
                  pl.BlockSpec((None, K, tn), lambda j, i: (layer, 0, j)),
                  pl.BlockSpec((None, K, tn), lambda j, i: (layer, 0, j)),
                  pl.BlockSpec((None, 3, tn), lambda j, i: (layer, 0, j)),
                  pl.BlockSpec((None, 1, tn), lambda j, i: (layer, 0, j))],
        out_specs=pl.BlockSpec((seq, tn), lambda j, i: (i, j)),
        scratch_shapes=[pltpu.VMEM((K, tn), BF16), pltpu.VMEM((K, tn), BF16)],
        compiler_params=_cparams(("arbitrary", "arbitrary")),
        name="ffn_up",
    )(xb, wu, wg, cw, cb)


def _dilated_bias_table():
    r = np.arange(ATT_BLK)[None, :]
    c = np.arange(ATT_BLK)[:, None]
    tables = []
    for dl in range(4):
        dist = dl * ATT_BLK + r - c
        cnt = np.zeros_like(dist)
        for (w, d) in DILATED_PATTERNS:
            cnt += ((dist >= 0) & (dist % d == 0) & (dist <= w)).astype(dist.dtype)
        with np.errstate(divide="ignore"):
            tables.append(np.where(cnt > 0, np.log(np.maximum(cnt, 1)), NEG))
    return np.stack(tables).astype(np.float32)


def _head_cols(g):
    return slice(g * HEAD_DIM, (g + 1) * HEAD_DIM)


def _softmax_init():
    T = ATT_BLK
    return jnp.full((1, T), NEG, F32), jnp.zeros((1, T), F32), jnp.zeros((HEAD_DIM, T), F32)


def _softmax_steps(qs, kbs, vbs, biases, carries, scale):
    n = len(qs)
    ss = [_dot_nt(kbs[g], qs[g]) * scale + biases[g] for g in range(n)]
    mid = []
    for g in range(n):
        m, l, acc = carries[g]
        m_new = jnp.maximum(m, jnp.max(ss[g], axis=0, keepdims=True))
        a = jnp.exp(m - m_new)
        p = jnp.exp(ss[g] - m_new)
        l = a * l + jnp.sum(p, axis=0, keepdims=True)
        mid.append((m_new, l, a, p.astype(BF16)))
    out = []
    for g in range(n):
        m_new, l, a, p = mid[g]
        out.append((m_new, l, a * carries[g][2] + _dot_tn(vbs[g], p)))
    return tuple(out)


def _softmax_finish(carry, dtype):
    m, l, acc = carry
    return (acc / l).T.astype(dtype)


def _attn_a_kernel(q_ref, k_ref, v_ref, bias_ref, o_ref, *, scale, heads):
    S = q_ref.shape[0]
    T = ATT_BLK
    for i in range(S // T):
        rows = slice(i * T, (i + 1) * T)
        qs = [q_ref[rows, _head_cols(g)] for g in range(heads)]

        def step(j, bias, carries):
            off = j * T if isinstance(j, int) else pl.multiple_of(j * T, T)
            return _softmax_steps(qs, [k_ref[pl.ds(off, T), _head_cols(g)] for g in range(heads)],
                                  [v_ref[pl.ds(off, T), _head_cols(g)] for g in range(heads)],
                                  [bias] * heads, carries, scale)

        carries = tuple(_softmax_init() for _ in range(heads))
        for dl in range(min(3, i + 1)):
            carries = step(i - dl, bias_ref[dl], carries)
        if i >= 3:
            far = bias_ref[3]
            carries = lax.fori_loop(0, i - 2, lambda j, c: step(j, far, c), carries)
        for g in range(heads):
            o_ref[rows, _head_cols(g)] = _softmax_finish(carries[g], o_ref.dtype)


def _attn_a(qkv, bias, batch, seq):
    G = ATT_HEADS
    kern = functools.partial(_attn_a_kernel, scale=HEAD_DIM ** -0.5, heads=G)
    HB = N_HEADS_A // G
    W = G * HEAD_DIM
    return pl.pallas_call(
        kern,
        out_shape=jax.ShapeDtypeStruct((batch * seq, W_A), BF16),
        grid=(batch, HB),
        in_specs=[pl.BlockSpec((seq, W), lambda b, h: (b, h)),
                  pl.BlockSpec((seq, W), lambda b, h: (b, HB + h)),
                  pl.BlockSpec((seq, W), lambda b, h: (b, 2 * HB + h)),
                  pl.BlockSpec((4, ATT_BLK, ATT_BLK), lambda b, h: (0, 0, 0))],
        out_specs=pl.BlockSpec((seq, W), lambda b, h: (b, h)),
        compiler_params=_cparams(("parallel", "parallel")),
        name="attn_dilated",
    )(qkv, qkv, qkv, bias)


def _moba_mean_matrix(seq):
    assert seq // MOBA_BLOCK <= MOBA_ROWS
    m = np.zeros((MOBA_ROWS, seq), np.float32)
    for j in range(seq // MOBA_BLOCK):
        m[j, j * MOBA_BLOCK:(j + 1) * MOBA_BLOCK] = 1.0 / MOBA_BLOCK
    return m


def _moba_select_bias(q, kmean3, i, blk):
    km_h, km_m, km_l = kmean3
    valid = blk < i
    if i > MOBA_TOPK:
        gate = _dot_nt(km_h, q) + _dot_nt(km_m, q) + _dot_nt(km_l, q)
        rank = jnp.zeros(gate.shape, jnp.int32)
        for jp in range(i):
            gj = gate[jp:jp + 1, :]
            beats = (gj > gate) | ((gj == gate) & (jp < blk))
            rank = rank + beats.astype(jnp.int32)
        sel = valid & (rank < MOBA_TOPK)
    else:
        sel = valid
    return jnp.where(sel, 0.0, NEG)


def _attn_b_kernel(q_ref, k_ref, v_ref, mean_ref, o_ref, *, scale, heads):
    S = q_ref.shape[0]
    T = ATT_BLK
    kmeans = [_split3(_dot(mean_ref[...], k_ref[:, _head_cols(g)])) for g in range(heads)]
    blk = lax.broadcasted_iota(jnp.int32, (MOBA_ROWS, T), 0)
    key = lax.broadcasted_iota(jnp.int32, (T, T), 0)
    qry = lax.broadcasted_iota(jnp.int32, (T, T), 1)
    causal_bias = jnp.where(key <= qry, 0.0, NEG)

    for i in range(S // T):
        rows = slice(i * T, (i + 1) * T)
        qs = [q_ref[rows, _head_cols(g)] for g in range(heads)]
        carries = _softmax_steps(qs, [k_ref[rows, _head_cols(g)] for g in range(heads)],
                                 [v_ref[rows, _head_cols(g)] for g in range(heads)],
                                 [causal_bias] * heads, [_softmax_init()] * heads, scale)
        if i > 0:
            selbias = [_moba_select_bias(qs[g], kmeans[g], i, blk) for g in range(heads)]

            def step(j, carries):
                off = pl.multiple_of(j * T, T)
                bjs = [jnp.sum(jnp.where(blk == j, selbias[g], 0.0), axis=0, keepdims=True)
                       for g in range(heads)]
                return _softmax_steps(qs, [k_ref[pl.ds(off, T), _head_cols(g)] for g in range(heads)],
                                      [v_ref[pl.ds(off, T), _head_cols(g)] for g in range(heads)],
                                      bjs, carries, scale)

            carries = lax.fori_loop(0, i, step, carries)
        for g in range(heads):
            o_ref[rows, _head_cols(g)] = _softmax_finish(carries[g], o_ref.dtype)


def _attn_b(qkv, mean_mat, batch, seq):
    G = ATT_HEADS
    kern = functools.partial(_attn_b_kernel, scale=HEAD_DIM ** -0.5, heads=G)
    HB = N_HEADS_B // G
    W = G * HEAD_DIM
    base = 3 * N_HEADS_A // G
    return pl.pallas_call(
        kern,
        out_shape=jax.ShapeDtypeStruct((batch * seq, W_B), BF16),
        grid=(batch, HB),
        in_specs=[pl.BlockSpec((seq, W), lambda b, h: (b, base + h)),
                  pl.BlockSpec((seq, W), lambda b, h: (b, base + HB + h)),
                  pl.BlockSpec((seq, W), lambda b, h: (b, base + 2 * HB + h)),
                  pl.BlockSpec((MOBA_ROWS, seq), lambda b, h: (0, 0))],
        out_specs=pl.BlockSpec((seq, W), lambda b, h: (b, h)),
        compiler_params=_cparams(("parallel", "parallel")),
        name="attn_moba",
    )(qkv, qkv, qkv, mean_mat)


def _attn_c_kernel(q_ref, k_ref, v_ref, o_ref, *, scale, heads):
    S = q_ref.shape[0]
    T = ATT_BLK
    key = lax.broadcasted_iota(jnp.int32, (T, T), 0)
    qry = lax.broadcasted_iota(jnp.int32, (T, T), 1)
    tri = jnp.where(qry >= key, 1.0, 0.0).astype(BF16)
    strict = key < qry

    def blocks(qs, row_sl, carries, diag):
        hs = range(heads)
        zs = [_dot_nt(k_ref[row_sl, _head_cols(g)], qs[g]) * scale for g in hs]
        sps = []
        for g in hs:
            sp = jnp.maximum(zs[g], 0.0) + jnp.log(1.0 + jnp.exp(-jnp.abs(zs[g])))
            if diag:
                sp = jnp.where(strict, sp, 0.0)
            sp_h = sp.astype(BF16)
            sps.append((sp_h, (sp - sp_h.astype(F32)).astype(BF16)))
        rcs = [_dot(tri, sps[g][0]) + _dot(tri, sps[g][1]) for g in hs]
        ws = []
        for g in hs:
            a = jnp.exp(zs[g] - (rcs[g] + carries[g][0]))
            if diag:
                a = jnp.where(strict, a, 0.0)
            ws.append(a.astype(BF16))
        return tuple((carries[g][0] + rcs[g][0:1, :],
                      carries[g][1] + _dot_tn(v_ref[row_sl, _head_cols(g)], ws[g])) for g in hs)

    for i in range(S // T):
        rows = slice(i * T, (i + 1) * T)
        qs = [q_ref[rows, _head_cols(g)] for g in range(heads)]
        init = (jnp.zeros((1, T), F32), jnp.zeros((HEAD_DIM, T), F32))
        carries = blocks(qs, rows, [init] * heads, True)
        if i > 0:
            def cond(state):
                return jnp.logical_and(state[0] < i, state[1] > 0)

            def step(state):
                t, _, carries = state
                off = pl.multiple_of((i - 1 - t) * T, T)
                new = blocks(qs, pl.ds(off, T), carries, False)
                cmin = new[0][0]
                for g in range(1, heads):
                    cmin = jnp.minimum(cmin, new[g][0])
                more = (jnp.min(cmin) < SB_UNDERFLOW).astype(jnp.int32)
                return t + 1, more, new

            _, _, carries = lax.while_loop(cond, step, (jnp.int32(0), jnp.int32(1), carries))
        for g in range(heads):
            o_ref[rows, _head_cols(g)] = carries[g][1].T.astype(o_ref.dtype)


def _attn_c(proj, batch, seq):
    G = ATT_HEADS
    kern = functools.partial(_attn_c_kernel, scale=HEAD_DIM ** -0.5, heads=G)
    HB = N_HEADS_C // G
    W = G * HEAD_DIM
    return pl.pallas_call(
        kern,
        out_shape=jax.ShapeDtypeStruct((batch * seq, W_C), BF16),
        grid=(batch, HB),
        in_specs=[pl.BlockSpec((seq, W), lambda b, h: (b, h)),
                  pl.BlockSpec((seq, W), lambda b, h: (b, HB + h)),
                  pl.BlockSpec((seq, W), lambda b, h: (b, 2 * HB + h))],
        out_specs=pl.BlockSpec((seq, W), lambda b, h: (b, h)),
        compiler_params=_cparams(("parallel", "parallel")),
        name="attn_stickbreak",
    )(proj, proj, proj)


def _conv4_silu(x, cw, cb):
    row = lax.broadcasted_iota(jnp.int32, x.shape, 0)
    y = cw[3:4, :] * x + cb
    for d in (1, 2, 3):
        xs = jnp.where(row >= d, pltpu.roll(x, d, 0), 0.0)
        y = y + cw[3 - d:4 - d, :] * xs
    return y * jax.nn.sigmoid(y)


def _mlstm_kernel(q_ref, k_ref, v_ref, o_ref, g_ref, cwq_ref, cbq_ref, cwk_ref, cbk_ref,
                  bif_ref, ng_ref, y_ref, qs_ref, ks_ref):
    h = pl.program_id(1)
    S = q_ref.shape[0]
    L = MLSTM_L
    qs_ref[...] = _conv4_silu(q_ref[...].astype(F32), cwq_ref[...], cbq_ref[...]).astype(BF16)
    ks_ref[...] = (_conv4_silu(k_ref[...].astype(F32), cwk_ref[...], cbk_ref[...])
                   * (DQK_D ** -0.5)).astype(BF16)

    row = lax.broadcasted_iota(jnp.int32, (L, L), 0)
    col = lax.broadcasted_iota(jnp.int32, (L, L), 1)
    lower = row >= col
    tril = jnp.where(lower, 1.0, 0.0).astype(BF16)
    lane = lax.broadcasted_iota(jnp.int32, (L, LANE), 1)
    sub = lax.broadcasted_iota(jnp.int32, (LANE, L), 0)
    ng = ng_ref[...]

    def chunk(ci, carry):
        Ct, n, m = carry
        off = pl.multiple_of(ci * L, L)
        gates = g_ref[pl.ds(off, L), :] + bif_ref[...]
        lsig = jnp.minimum(gates, 0.0) - jnp.log(1.0 + jnp.exp(-jnp.abs(gates)))
        li_c = jnp.sum(jnp.where(lane == h, gates, 0.0), axis=-1, keepdims=True)
        lf_c = jnp.sum(jnp.where(lane == h + N_HEADS_D, lsig, 0.0), axis=-1, keepdims=True)
        gates_t = gates.T
        lsig_t = lsig.T
        li_r = jnp.sum(jnp.where(sub == h, gates_t, 0.0), axis=0, keepdims=True)
        lf_r = jnp.sum(jnp.where(sub == h + N_HEADS_D, lsig_t, 0.0), axis=0, keepdims=True)
        fh, fm, fl = _split3(jnp.broadcast_to(lf_c, (L, LANE)))
        b_c = (_dot(tril, fh) + _dot(tril, fm) + _dot(tril, fl))[:, 0:1]
        rh, rm, rl = _split3(jnp.broadcast_to(lf_r, (8, L)))
        b_r = (_dot_nt(rh, tril) + _dot_nt(rm, tril) + _dot_nt(rl, tril))[0:1, :]
        g = jnp.sum(lf_c, axis=0, keepdims=True)

        qc = qs_ref[pl.ds(off, L), :]
        kc = ks_ref[pl.ds(off, L), :]
        vc = v_ref[pl.ds(off, L), :]

        dm = jnp.where(lower, b_c - b_r + li_r, NEG)
        inter = b_c + m
        m_t = jnp.maximum(inter, jnp.max(dm, axis=-1, keepdims=True))
        p = jnp.exp(dm - m_t)
        w_inter = jnp.exp(inter - m_t)
        sqk = _dot_nt(qc, kc) * p
        qf = qc.astype(F32)
        num = w_inter * _dot(qc, Ct.astype(BF16)) + _dot(sqk.astype(BF16), vc)
        den = w_inter * jnp.sum(qf * n, axis=-1, keepdims=True) + jnp.sum(sqk, axis=-1, keepdims=True)
        hh = num / jnp.maximum(jnp.abs(den), jnp.exp(-m_t))

        dec_c = g - b_c + li_c
        m_new = jnp.maximum(g + m, jnp.max(dec_c, axis=0, keepdims=True))
        wk = jnp.exp(dec_c - m_new)
        sc = jnp.exp(g + m - m_new)
        kw = kc.astype(F32) * wk
        Ct_new = sc * Ct + _dot_tn(kw.astype(BF16), vc)
        n_new = sc * n + jnp.sum(kw, axis=0, keepdims=True)

        mu = jnp.mean(hh, axis=-1, keepdims=True)
        hc = hh - mu
        var = jnp.mean(hc * hc, axis=-1, keepdims=True)
        hn = hc * lax.rsqrt(var + LN_EPS) * ng
        og = o_ref[pl.ds(off, L), :].astype(F32)
        y_ref[pl.ds(off, L), :] = (jax.nn.sigmoid(og) * hn).astype(y_ref.dtype)
        return Ct_new, n_new, m_new

    init = (jnp.zeros((DQK_D, DV_D), F32), jnp.zeros((1, DQK_D), F32), jnp.zeros((1, 1), F32))
    lax.fori_loop(0, S // L, chunk, init)


def _mlstm(proj, gates, conv_w, conv_b, b_if, norm_g, batch, seq):
    H = N_HEADS_D
    qk0 = 3 * W_C // DQK_D
    v0 = (3 * W_C + 2 * QK_D) // DV_D
    o0 = v0 + H
    return pl.pallas_call(
        _mlstm_kernel,
        out_shape=jax.ShapeDtypeStruct((batch * seq, W_D), BF16),
        grid=(batch, H),
        in_specs=[pl.BlockSpec((seq, DQK_D), lambda b, h: (b, qk0 + h)),
                  pl.BlockSpec((seq, DQK_D), lambda b, h: (b, qk0 + H + h)),
                  pl.BlockSpec((seq, DV_D), lambda b, h: (b, v0 + h)),
                  pl.BlockSpec((seq, DV_D), lambda b, h: (b, o0 + h)),
                  pl.BlockSpec((seq, GATE_PAD), lambda b, h: (b, 0)),
                  pl.BlockSpec((4, DQK_D), lambda b, h: (0, h)),
                  pl.BlockSpec((1, DQK_D), lambda b, h: (0, h)),
                  pl.BlockSpec((4, DQK_D), lambda b, h: (0, H + h)),
                  pl.BlockSpec((1, DQK_D), lambda b, h: (0, H + h)),
                  pl.BlockSpec((1, GATE_PAD), lambda b, h: (0, 0)),
                  pl.BlockSpec((1, DV_D), lambda b, h: (0, h))],
        out_specs=pl.BlockSpec((seq, DV_D), lambda b, h: (b, h)),
        scratch_shapes=[pltpu.VMEM((seq, DQK_D), BF16), pltpu.VMEM((seq, DQK_D), BF16)],
        compiler_params=_cparams(("parallel", "arbitrary")),
        name="mlstm",
    )(proj, proj, proj, proj, gates, conv_w, conv_b, conv_w, conv_b, b_if, norm_g)


def _conv_ffn_block(res, xb, w_up, w_gate, conv_w, conv_b, w_down_bf16, layer, ln_g, ln_b, seq, final):
    h = _ffn_up(xb, w_up, w_gate, conv_w, conv_b.reshape(DEPTH, 1, D_FF), layer, seq)
    z = _ffn_down_resid(h, w_down_bf16, res, layer)
    return _layer_norm(z, ln_g, ln_b, final=final)


def kernel(x, w_in_ab, w_out_ab, w_in_cd, b_if_cd, conv_cd, conv_cd_b, norm_cd_g, w_out_cd,
           ffn_w_up, ffn_w_gate, ffn_conv, ffn_conv_b, ffn_w_down, ln_g, ln_b):
    B, S, D = x.shape
    x = x.reshape(B * S, D)
    xb = x.astype(BF16)
    w_down = ffn_w_down.astype(BF16)
    ffn = (ffn_w_up, ffn_w_gate, ffn_conv, ffn_conv_b, w_down)

    qkv = _matmul(xb, w_in_ab, 0, IN_MAIN, BF16)
    ya = _attn_a(qkv, jnp.asarray(_dilated_bias_table()), B, S)
    yb = _attn_b(qkv, jnp.asarray(_moba_mean_matrix(S), BF16), B, S)
    z = _out_proj_resid(ya, yb, w_out_ab, (x,))
    xb, res = _layer_norm(z, ln_g[0, 0], ln_b[0, 0])
    xb, res = _conv_ffn_block(res, xb, *ffn, 0, ln_g[0, 1], ln_b[0, 1], S, False)

    w_in_t = jnp.swapaxes(w_in_cd, 1, 2)
    proj = _matmul(xb, w_in_t, 0, IN_MAIN, BF16, w_transposed=True)
    gates = _matmul(xb, w_in_t, IN_MAIN, GATE_PAD, F32, w_transposed=True)
    yc = _attn_c(proj, B, S)
    b_if = jnp.pad(b_if_cd[0], ((0, GATE_PAD - 2 * N_HEADS_D),)).reshape(1, GATE_PAD)
    yd = _mlstm(proj, gates, conv_cd[0], conv_cd_b[0].reshape(1, 2 * QK_D), b_if,
                norm_cd_g[0].reshape(1, W_D), B, S)
    z = _out_proj_resid(yc, yd, w_out_cd, res)
    xb, res = _layer_norm(z, ln_g[1, 0], ln_b[1, 0])
    out = _conv_ffn_block(res, xb, *ffn, 1, ln_g[1, 1], ln_b[1, 1], S, True)
    return out.reshape(B, S, D)
```

```python
import functools

import numpy as np
import jax
import jax.numpy as jnp
from jax import lax
from jax.experimental import pallas as pl
from jax.experimental.pallas import tpu as pltpu

F32 = jnp.float32
BF16 = jnp.bfloat16

D_MODEL = 4096
DEPTH = 2
HEAD_DIM = 128
N_HEADS_A = 16
N_HEADS_B = 16
N_HEADS_C = 16
N_HEADS_D = 8
DQK_D = 128
DV_D = 256
D_FF = 11008
LN_EPS = 1e-5
ALPHA = (2 * DEPTH) ** 0.25
MOBA_BLOCK = 256
MOBA_TOPK = 3
MOBA_ROWS = 16
DILATED_PATTERNS = ((128, 1), (512, 4), (2048, 16))

W_A = N_HEADS_A * HEAD_DIM
W_B = N_HEADS_B * HEAD_DIM
W_C = N_HEADS_C * HEAD_DIM
QK_D = N_HEADS_D * DQK_D
W_D = N_HEADS_D * DV_D
IN_MAIN = 12288
GATE_PAD = 128

LANE = 128
ATT_BLK = 256
ATT_HEADS = 8
MLSTM_L = 256
MLSTM_HEADS = 4
FFN_TN = 256
FFN_ROWS = 512
NEG = -1e30
SB_UNDERFLOW = 104.0

VMEM_LIMIT = 56 * 1024 * 1024


def _cparams(sem):
    return pltpu.CompilerParams(dimension_semantics=sem, vmem_limit_bytes=VMEM_LIMIT)


def _dot(a, b):
    return jnp.dot(a, b, preferred_element_type=F32)


def _dot_nt(a, b):
    return lax.dot_general(a, b, (((1,), (1,)), ((), ())), preferred_element_type=F32)


def _dot_tn(a, b):
    return lax.dot_general(a, b, (((0,), (0,)), ((), ())), preferred_element_type=F32)


def _split3(x):
    h = x.astype(BF16)
    r = x - h.astype(F32)
    m = r.astype(BF16)
    l = (r - m.astype(F32)).astype(BF16)
    return h, m, l


def _mm_wres_kernel(a_ref, w_ref, o_ref, wb_ref, *, w_transposed):
    @pl.when(pl.program_id(1) == 0)
    def _():
        w = w_ref[...]
        wb_ref[...] = (w.T if w_transposed else w).astype(BF16)

    o_ref[...] = _dot(a_ref[...], wb_ref[...]).astype(o_ref.dtype)


def _matmul(a, w, col0, n_cols, out_dtype, w_transposed=False, tm=1024, tn=512):
    M, K = a.shape
    tn = min(tn, n_cols)
    j0 = col0 // tn
    assert col0 % tn == 0
    if w_transposed:
        w_spec = pl.BlockSpec((None, tn, K), lambda j, i: (0, j0 + j, 0))
    else:
        w_spec = pl.BlockSpec((None, K, tn), lambda j, i: (0, 0, j0 + j))
    return pl.pallas_call(
        functools.partial(_mm_wres_kernel, w_transposed=w_transposed),
        out_shape=jax.ShapeDtypeStruct((M, n_cols), out_dtype),
        grid=(n_cols // tn, M // tm),
        in_specs=[pl.BlockSpec((tm, K), lambda j, i: (i, 0)), w_spec],
        out_specs=pl.BlockSpec((tm, tn), lambda j, i: (i, j)),
        scratch_shapes=[pltpu.VMEM((K, tn), BF16)],
        compiler_params=_cparams(("arbitrary", "arbitrary")),
        name="matmul",
    )(a, w)


def _resid_tile(res_refs):
    if len(res_refs) == 1:
        return res_refs[0][...]
    z_ref, mu_ref, rs_ref, g_ref, b_ref = res_refs
    return (z_ref[...] - mu_ref[...]) * rs_ref[...] * g_ref[...] + b_ref[...]


def _resid_specs(res, tm, tn, ij):
    tile = pl.BlockSpec((tm, tn), lambda *g: ij(*g))
    if len(res) == 1:
        return [tile]
    row = pl.BlockSpec((tm, 1), lambda *g: (ij(*g)[0], 0))
    col = pl.BlockSpec((1, tn), lambda *g: (0, ij(*g)[1]))
    return [tile, row, row, col, col]


def _mm2_resid_kernel(a1_ref, a2_ref, w1_ref, w2_ref, *refs):
    *res_refs, o_ref, wb1_ref, wb2_ref = refs

    @pl.when(pl.program_id(1) == 0)
    def _():
        wb1_ref[...] = w1_ref[...].astype(BF16)
        wb2_ref[...] = w2_ref[...].astype(BF16)

    y = _dot(a1_ref[...], wb1_ref[...]) + _dot(a2_ref[...], wb2_ref[...])
    o_ref[...] = ALPHA * _resid_tile(res_refs) + y


def _out_proj_resid(a1, a2, w, res, tm=1024, tn=512):
    M, K1 = a1.shape
    K2 = a2.shape[1]
    assert K1 == K2
    N = w.shape[2]
    return pl.pallas_call(
        _mm2_resid_kernel,
        out_shape=jax.ShapeDtypeStruct((M, N), F32),
        grid=(N // tn, M // tm),
        in_specs=[pl.BlockSpec((tm, K1), lambda j, i: (i, 0)),
                  pl.BlockSpec((tm, K2), lambda j, i: (i, 0)),
                  pl.BlockSpec((None, K1, tn), lambda j, i: (0, 0, j)),
                  pl.BlockSpec((None, K2, tn), lambda j, i: (0, 1, j)),
                  *_resid_specs(res, tm, tn, lambda j, i: (i, j))],
        out_specs=pl.BlockSpec((tm, tn), lambda j, i: (i, j)),
        scratch_shapes=[pltpu.VMEM((K1, tn), BF16), pltpu.VMEM((K2, tn), BF16)],
        compiler_params=_cparams(("arbitrary", "arbitrary")),
        name="out_proj_resid",
    )(a1, a2, w, w, *res)


def _mm_resid_kernel(a_ref, w_ref, *refs):
    *res_refs, o_ref = refs
    o_ref[...] = ALPHA * _resid_tile(res_refs) + _dot(a_ref[...], w_ref[...])


def _ffn_down_resid(h, w, res, layer, tm=512, tn=512):
    M, K = h.shape
    N = w.shape[2]
    return pl.pallas_call(
        _mm_resid_kernel,
        out_shape=jax.ShapeDtypeStruct((M, N), F32),
        grid=(M // tm, N // tn),
        in_specs=[pl.BlockSpec((tm, K), lambda i, j: (i, 0)),
                  pl.BlockSpec((None, K, tn), lambda i, j: (layer, 0, j)),
                  *_resid_specs(res, tm, tn, lambda i, j: (i, j))],
        out_specs=pl.BlockSpec((tm, tn), lambda i, j: (i, j)),
        compiler_params=_cparams(("parallel", "arbitrary")),
        name="ffn_down_resid",
    )(h, w, *res)


def _ln_stats(z):
    mu = jnp.mean(z, axis=-1, keepdims=True)
    zc = z - mu
    var = jnp.mean(zc * zc, axis=-1, keepdims=True)
    return mu, lax.rsqrt(var + LN_EPS)


def _ln_bf16_kernel(z_ref, g_ref, b_ref, ob_ref, mu_ref, rs_ref):
    z = z_ref[...]
    mu, rs = _ln_stats(z)
    ob_ref[...] = ((z - mu) * rs * g_ref[...] + b_ref[...]).astype(BF16)
    mu_ref[...] = mu
    rs_ref[...] = rs


def _ln_f32_kernel(z_ref, g_ref, b_ref, o_ref):
    z = z_ref[...]
    mu, rs = _ln_stats(z)
    o_ref[...] = (z - mu) * rs * g_ref[...] + b_ref[...]


def _layer_norm(z, g, b, final=False, tr=256):
    M, N = z.shape
    g = g.reshape(1, N)
    b = b.reshape(1, N)
    rows = pl.BlockSpec((tr, N), lambda i: (i, 0))
    stat = pl.BlockSpec((tr, 1), lambda i: (i, 0))
    vec = pl.BlockSpec((1, N), lambda i: (0, 0))
    if final:
        return pl.pallas_call(
            _ln_f32_kernel,
            out_shape=jax.ShapeDtypeStruct((M, N), F32),
            grid=(M // tr,), in_specs=[rows, vec, vec], out_specs=rows,
            compiler_params=_cparams(("parallel",)), name="layer_norm_out",
        )(z, g, b)
    xb, mu, rs = pl.pallas_call(
        _ln_bf16_kernel,
        out_shape=(jax.ShapeDtypeStruct((M, N), BF16), jax.ShapeDtypeStruct((M, 1), F32),
                   jax.ShapeDtypeStruct((M, 1), F32)),
        grid=(M // tr,), in_specs=[rows, vec, vec], out_specs=(rows, stat, stat),
        compiler_params=_cparams(("parallel",)), name="layer_norm",
    )(z, g, b)
    return xb, (z, mu, rs, g, b)


def _ffn_up_kernel(x_ref, wu_ref, wg_ref, cw_ref, cb_ref, h_ref, wub_ref, wgb_ref):
    @pl.when(pl.program_id(1) == 0)
    def _():
        wub_ref[...] = wu_ref[...].astype(BF16)
        wgb_ref[...] = wg_ref[...].astype(BF16)

    cw = cw_ref[...]
    cb = cb_ref[...]
    R = FFN_ROWS
    row = lax.broadcasted_iota(jnp.int32, (R, cw.shape[1]), 0)
    tail = jnp.zeros((8, cw.shape[1]), F32)
    for c in range(x_ref.shape[0] // R):
        x = x_ref[c * R:(c + 1) * R, :]
        g = _dot(x, wgb_ref[...])
        u = _dot(x, wub_ref[...])
        gm1 = tail[7:8, :]
        gm2 = tail[6:7, :]
        g1 = jnp.where(row == 0, gm1, pltpu.roll(g, 1, 0))
        g2 = jnp.where(row == 0, gm2, jnp.where(row == 1, gm1, pltpu.roll(g, 2, 0)))
        tail = g[R - 8:, :]
        gc = cw[0:1, :] * g2 + cw[1:2, :] * g1 + cw[2:3, :] * g + cb
        h_ref[c * R:(c + 1) * R, :] = (gc * jax.nn.sigmoid(gc) * u).astype(h_ref.dtype)


def _ffn_up(xb, wu, wg, cw, cb, layer, seq, tn=FFN_TN):
    M, K = xb.shape
    N = wu.shape[2]
    return pl.pallas_call(
        _ffn_up_kernel,
        out_shape=jax.ShapeDtypeStruct((M, N), BF16),
        grid=(N // tn, M // seq),
        in_specs=[pl.BlockSpec((seq, K), lambda j, i: (i, 0)),
                  pl.BlockSpec((None, K, tn), lambda j, i: (layer, 0, j)),
                  pl.BlockSpec((None, K, tn), lambda j, i: (layer, 0, j)),
                  pl.BlockSpec((None, 3, tn), lambda j, i: (layer, 0, j)),
                  pl.BlockSpec((None, 1, tn), lambda j, i: (layer, 0, j))],
        out_specs=pl.BlockSpec((seq, tn), lambda j, i: (i, j)),
        scratch_shapes=[pltpu.VMEM((K, tn), BF16), pltpu.VMEM((K, tn), BF16)],
        compiler_params=_cparams(("arbitrary", "arbitrary")),
        name="ffn_up",
    )(xb, wu, wg, cw, cb)


def _dilated_bias_table():
    r = np.arange(ATT_BLK)[None, :]
    c = np.arange(ATT_BLK)[:, None]
    tables = []
    for dl in range(4):
        dist = dl * ATT_BLK + r - c
        cnt = np.zeros_like(dist)
        for (w, d) in DILATED_PATTERNS:
            cnt += ((dist >= 0) & (dist % d == 0) & (dist <= w)).astype(dist.dtype)
        with np.errstate(divide="ignore"):
            tables.append(np.where(cnt > 0, np.log(np.maximum(cnt, 1)), NEG))
    return np.stack(tables).astype(np.float32)


def _head_cols(g):
    return slice(g * HEAD_DIM, (g + 1) * HEAD_DIM)


def _softmax_init():
    T = ATT_BLK
    return jnp.full((1, T), NEG, F32), jnp.zeros((1, T), F32), jnp.zeros((HEAD_DIM, T), F32)


def _softmax_steps(qs, kbs, vbs, biases, carries, scale):
    n = len(qs)
    ss = [_dot_nt(kbs[g], qs[g]) * scale + biases[g] for g in range(n)]
    mid = []
    for g in range(n):
        m, l, acc = carries[g]
        m_new = jnp.maximum(m, jnp.max(ss[g], axis=0, keepdims=True))
        a = jnp.exp(m - m_new)
        p = jnp.exp(ss[g] - m_new)
        l = a * l + jnp.sum(p, axis=0, keepdims=True)
        mid.append((m_new, l, a, p.astype(BF16)))
    out = []
    for g in range(n):
        m_new, l, a, p = mid[g]
        out.append((m_new, l, a * carries[g][2] + _dot_tn(vbs[g], p)))
    return tuple(out)


def _softmax_finish(carry, dtype):
    m, l, acc = carry
    return (acc / l).T.astype(dtype)


def _attn_a_kernel(q_ref, k_ref, v_ref, bias_ref, o_ref, *, scale, heads):
    S = q_ref.shape[0]
    T = ATT_BLK
    for i in range(S // T):
        rows = slice(i * T, (i + 1) * T)
        qs = [q_ref[rows, _head_cols(g)] for g in range(heads)]

        def step(j, bias, carries):
            off = j * T if isinstance(j, int) else pl.multiple_of(j * T, T)
            return _softmax_steps(qs, [k_ref[pl.ds(off, T), _head_cols(g)] for g in range(heads)],
                                  [v_ref[pl.ds(off, T), _head_cols(g)] for g in range(heads)],
                                  [bias] * heads, carries, scale)

        carries = tuple(_softmax_init() for _ in range(heads))
        for dl in range(min(3, i + 1)):
            carries = step(i - dl, bias_ref[dl], carries)
        if i >= 3:
            far = bias_ref[3]
            carries = lax.fori_loop(0, i - 2, lambda j, c: step(j, far, c), carries)
        for g in range(heads):
            o_ref[rows, _head_cols(g)] = _softmax_finish(carries[g], o_ref.dtype)


def _attn_a(qkv, bias, batch, seq):
    G = ATT_HEADS
    kern = functools.partial(_attn_a_kernel, scale=HEAD_DIM ** -0.5, heads=G)
    HB = N_HEADS_A // G
    W = G * HEAD_DIM
    return pl.pallas_call(
        kern,
        out_shape=jax.ShapeDtypeStruct((batch * seq, W_A), BF16),
        grid=(batch, HB),
        in_specs=[pl.BlockSpec((seq, W), lambda b, h: (b, h)),
                  pl.BlockSpec((seq, W), lambda b, h: (b, HB + h)),
                  pl.BlockSpec((seq, W), lambda b, h: (b, 2 * HB + h)),
                  pl.BlockSpec((4, ATT_BLK, ATT_BLK), lambda b, h: (0, 0, 0))],
        out_specs=pl.BlockSpec((seq, W), lambda b, h: (b, h)),
        compiler_params=_cparams(("parallel", "parallel")),
        name="attn_dilated",
    )(qkv, qkv, qkv, bias)


def _moba_mean_matrix(seq):
    assert seq // MOBA_BLOCK <= MOBA_ROWS
    m = np.zeros((MOBA_ROWS, seq), np.float32)
    for j in range(seq // MOBA_BLOCK):
        m[j, j * MOBA_BLOCK:(j + 1) * MOBA_BLOCK] = 1.0 / MOBA_BLOCK
    return m


def _moba_select_bias(q, kmean3, i, blk):
    km_h, km_m, km_l = kmean3
    valid = blk < i
    if i > MOBA_TOPK:
        gate = _dot_nt(km_h, q) + _dot_nt(km_m, q) + _dot_nt(km_l, q)
        rank = jnp.zeros(gate.shape, jnp.int32)
        for jp in range(i):
            gj = gate[jp:jp + 1, :]
            beats = (gj > gate) | ((gj == gate) & (jp < blk))
            rank = rank + beats.astype(jnp.int32)
        sel = valid & (rank < MOBA_TOPK)
    else:
        sel = valid
    return jnp.where(sel, 0.0, NEG)


def _attn_b_kernel(q_ref, k_ref, v_ref, mean_ref, o_ref, *, scale, heads):
    S = q_ref.shape[0]
    T = ATT_BLK
    kmeans = [_split3(_dot(mean_ref[...], k_ref[:, _head_cols(g)])) for g in range(heads)]
    blk = lax.broadcasted_iota(jnp.int32, (MOBA_ROWS, T), 0)
    key = lax.broadcasted_iota(jnp.int32, (T, T), 0)
    qry = lax.broadcasted_iota(jnp.int32, (T, T), 1)
    causal_bias = jnp.where(key <= qry, 0.0, NEG)

    for i in range(S // T):
        rows = slice(i * T, (i + 1) * T)
        qs = [q_ref[rows, _head_cols(g)] for g in range(heads)]
        carries = _softmax_steps(qs, [k_ref[rows, _head_cols(g)] for g in range(heads)],
                                 [v_ref[rows, _head_cols(g)] for g in range(heads)],
                                 [causal_bias] * heads, [_softmax_init()] * heads, scale)
        if i > 0:
            selbias = [_moba_select_bias(qs[g], kmeans[g], i, blk) for g in range(heads)]

            def step(j, carries):
                off = pl.multiple_of(j * T, T)
                bjs = [jnp.sum(jnp.where(blk == j, selbias[g], 0.0), axis=0, keepdims=True)
                       for g in range(heads)]
                return _softmax_steps(qs, [k_ref[pl.ds(off, T), _head_cols(g)] for g in range(heads)],
                                      [v_ref[pl.ds(off, T), _head_cols(g)] for g in range(heads)],
                                      bjs, carries, scale)

            carries = lax.fori_loop(0, i, step, carries)
        for g in range(heads):
            o_ref[rows, _head_cols(g)] = _softmax_finish(carries[g], o_ref.dtype)


def _attn_b(qkv, mean_mat, batch, seq):
    G = ATT_HEADS
    kern = functools.partial(_attn_b_kernel, scale=HEAD_DIM ** -0.5, heads=G)
    HB = N_HEADS_B // G
    W = G * HEAD_DIM
    base = 3 * N_HEADS_A // G
    return pl.pallas_call(
        kern,
        out_shape=jax.ShapeDtypeStruct((batch * seq, W_B), BF16),
        grid=(batch, HB),
        in_specs=[pl.BlockSpec((seq, W), lambda b, h: (b, base + h)),
                  pl.BlockSpec((seq, W), lambda b, h: (b, base + HB + h)),
                  pl.BlockSpec((seq, W), lambda b, h: (b, base + 2 * HB + h)),
                  pl.BlockSpec((MOBA_ROWS, seq), lambda b, h: (0, 0))],
        out_specs=pl.BlockSpec((seq, W), lambda b, h: (b, h)),
        compiler_params=_cparams(("parallel", "parallel")),
        name="attn_moba",
    )(qkv, qkv, qkv, mean_mat)


def _attn_c_kernel(q_ref, k_ref, v_ref, o_ref, *, scale, heads):
    S = q_ref.shape[0]
    T = ATT_BLK
    key = lax.broadcasted_iota(jnp.int32, (T, T), 0)
    qry = lax.broadcasted_iota(jnp.int32, (T, T), 1)
    tri = jnp.where(qry >= key, 1.0, 0.0).astype(BF16)
    strict = key < qry

    def blocks(qs, row_sl, carries, diag):
        hs = range(heads)
        zs = [_dot_nt(k_ref[row_sl, _head_cols(g)], qs[g]) * scale for g in hs]
        sps = []
        for g in hs:
            sp = jnp.maximum(zs[g], 0.0) + jnp.log(1.0 + jnp.exp(-jnp.abs(zs[g])))
            if diag:
                sp = jnp.where(strict, sp, 0.0)
            sp_h = sp.astype(BF16)
            sps.append((sp_h, (sp - sp_h.astype(F32)).astype(BF16)))
        rcs = [_dot(tri, sps[g][0]) + _dot(tri, sps[g][1]) for g in hs]
        ws = []
        for g in hs:
            a = jnp.exp(zs[g] - (rcs[g] + carries[g][0]))
            if diag:
                a = jnp.where(strict, a, 0.0)
            ws.append(a.astype(BF16))
        return tuple((carries[g][0] + rcs[g][0:1, :],
                      carries[g][1] + _dot_tn(v_ref[row_sl, _head_cols(g)], ws[g])) for g in hs)

    for i in range(S // T):
        rows = slice(i * T, (i + 1) * T)
        qs = [q_ref[rows, _head_cols(g)] for g in range(heads)]
        init = (jnp.zeros((1, T), F32), jnp.zeros((HEAD_DIM, T), F32))
        carries = blocks(qs, rows, [init] * heads, True)
        if i > 0:
            def cond(state):
                return jnp.logical_and(state[0] < i, state[1] > 0)

            def step(state):
                t, _, carries = state
                off = pl.multiple_of((i - 1 - t) * T, T)
                new = blocks(qs, pl.ds(off, T), carries, False)
                cmin = new[0][0]
                for g in range(1, heads):
                    cmin = jnp.minimum(cmin, new[g][0])
                more = (jnp.min(cmin) < SB_UNDERFLOW).astype(jnp.int32)
                return t + 1, more, new

            _, _, carries = lax.while_loop(cond, step, (jnp.int32(0), jnp.int32(1), carries))
        for g in range(heads):
            o_ref[rows, _head_cols(g)] = carries[g][1].T.astype(o_ref.dtype)


def _attn_c(proj, batch, seq):
    G = ATT_HEADS
    kern = functools.partial(_attn_c_kernel, scale=HEAD_DIM ** -0.5, heads=G)
    HB = N_HEADS_C // G
    W = G * HEAD_DIM
    return pl.pallas_call(
        kern,
        out_shape=jax.ShapeDtypeStruct((batch * seq, W_C), BF16),
        grid=(batch, HB),
        in_specs=[pl.BlockSpec((seq, W), lambda b, h: (b, h)),
                  pl.BlockSpec((seq, W), lambda b, h: (b, HB + h)),
                  pl.BlockSpec((seq, W), lambda b, h: (b, 2 * HB + h))],
        out_specs=pl.BlockSpec((seq, W), lambda b, h: (b, h)),
        compiler_params=_cparams(("parallel", "parallel")),
        name="attn_stickbreak",
    )(proj, proj, proj)


def _conv4_silu(x, cw, cb):
    row = lax.broadcasted_iota(jnp.int32, x.shape, 0)
    y = cw[3:4, :] * x + cb
    for d in (1, 2, 3):
        xs = jnp.where(row >= d, pltpu.roll(x, d, 0), 0.0)
        y = y + cw[3 - d:4 - d, :] * xs
    return y * jax.nn.sigmoid(y)


def _mlstm_kernel(q_ref, k_ref, v_ref, o_ref, g_ref, cwq_ref, cbq_ref, cwk_ref, cbk_ref,
                  bif_ref, ng_ref, y_ref, qs_ref, ks_ref, *, heads):
    h0 = pl.program_id(1) * heads
    S = q_ref.shape[0]
    L = MLSTM_L
    hs = range(heads)
    qs_ref[...] = _conv4_silu(q_ref[...].astype(F32), cwq_ref[...], cbq_ref[...]).astype(BF16)
    ks_ref[...] = (_conv4_silu(k_ref[...].astype(F32), cwk_ref[...], cbk_ref[...])
                   * (DQK_D ** -0.5)).astype(BF16)

    row = lax.broadcasted_iota(jnp.int32, (L, L), 0)
    col = lax.broadcasted_iota(jnp.int32, (L, L), 1)
    lower = row >= col
    tril = jnp.where(lower, 1.0, 0.0).astype(BF16)
    lane = lax.broadcasted_iota(jnp.int32, (L, LANE), 1)
    sub = lax.broadcasted_iota(jnp.int32, (LANE, L), 0)
    qk_cols = [slice(g * DQK_D, (g + 1) * DQK_D) for g in hs]
    v_cols = [slice(g * DV_D, (g + 1) * DV_D) for g in hs]

    def chunk(ci, carries):
        off = pl.multiple_of(ci * L, L)
        rows = pl.ds(off, L)
        gates = g_ref[rows, :] + bif_ref[...]
        lsig = jnp.minimum(gates, 0.0) - jnp.log(1.0 + jnp.exp(-jnp.abs(gates)))
        gates_t = gates.T
        lsig_t = lsig.T

        gv = []
        for g in hs:
            li_c = jnp.sum(jnp.where(lane == h0 + g, gates, 0.0), axis=-1, keepdims=True)
            lf_c = jnp.sum(jnp.where(lane == h0 + g + N_HEADS_D, lsig, 0.0), axis=-1, keepdims=True)
            li_r = jnp.sum(jnp.where(sub == h0 + g, gates_t, 0.0), axis=0, keepdims=True)
            lf_r = jnp.sum(jnp.where(sub == h0 + g + N_HEADS_D, lsig_t, 0.0), axis=0, keepdims=True)
            gv.append((li_c, lf_c, li_r, lf_r))
        cum = []
        for g in hs:
            li_c, lf_c, li_r, lf_r = gv[g]
            fh, fm, fl = _split3(jnp.broadcast_to(lf_c, (L, LANE)))
            b_c = (_dot(tril, fh) + _dot(tril, fm) + _dot(tril, fl))[:, 0:1]
            rh, rm, rl = _split3(jnp.broadcast_to(lf_r, (8, L)))
            b_r = (_dot_nt(rh, tril) + _dot_nt(rm, tril) + _dot_nt(rl, tril))[0:1, :]
            cum.append((b_c, b_r, jnp.sum(lf_c, axis=0, keepdims=True)))

        qcs = [qs_ref[rows, qk_cols[g]] for g in hs]
        kcs = [ks_ref[rows, qk_cols[g]] for g in hs]
        vcs = [v_ref[rows, v_cols[g]] for g in hs]
        s_raw = [_dot_nt(qcs[g], kcs[g]) for g in hs]
        inter_mm = [_dot(qcs[g], carries[g][0].astype(BF16)) for g in hs]

        mid = []
        for g in hs:
            Ct, n, m = carries[g]
            li_c, lf_c, li_r, lf_r = gv[g]
            b_c, b_r, tot = cum[g]
            dm = jnp.where(lower, b_c - b_r + li_r, NEG)
            inter = b_c + m
            m_t = jnp.maximum(inter, jnp.max(dm, axis=-1, keepdims=True))
            p = jnp.exp(dm - m_t)
            w_inter = jnp.exp(inter - m_t)
            sqk = s_raw[g] * p
            den = (w_inter * jnp.sum(qcs[g].astype(F32) * n, axis=-1, keepdims=True)
                   + jnp.sum(sqk, axis=-1, keepdims=True))
            mid.append((sqk.astype(BF16), w_inter, den, m_t))
        intra_mm = [_dot(mid[g][0], vcs[g]) for g in hs]

        new = []
        for g in hs:
            Ct, n, m = carries[g]
            li_c = gv[g][0]
            b_c, b_r, tot = cum[g]
            dec_c = tot - b_c + li_c
            m_new = jnp.maximum(tot + m, jnp.max(dec_c, axis=0, keepdims=True))
            wk = jnp.exp(dec_c - m_new)
            sc = jnp.exp(tot + m - m_new)
            kw = kcs[g].astype(F32) * wk
            new.append((sc, kw, m_new))
        state_mm = [_dot_tn(new[g][1].astype(BF16), vcs[g]) for g in hs]

        out = []
        for g in hs:
            Ct, n, m = carries[g]
            _, w_inter, den, m_t = mid[g]
            sc, kw, m_new = new[g]
            num = w_inter * inter_mm[g] + intra_mm[g]
            hh = num / jnp.maximum(jnp.abs(den), jnp.exp(-m_t))
            mu = jnp.mean(hh, axis=-1, keepdims=True)
            hc = hh - mu
            var = jnp.mean(hc * hc, axis=-1, keepdims=True)
            hn = hc * lax.rsqrt(var + LN_EPS) * ng_ref[:, v_cols[g]]
            og = o_ref[rows, v_cols[g]].astype(F32)
            y_ref[rows, v_cols[g]] = (jax.nn.sigmoid(og) * hn).astype(y_ref.dtype)
            out.append((sc * Ct + state_mm[g], sc * n + jnp.sum(kw, axis=0, keepdims=True), m_new))
        return tuple(out)

    init = (jnp.zeros((DQK_D, DV_D), F32), jnp.zeros((1, DQK_D), F32), jnp.zeros((1, 1), F32))
    lax.fori_loop(0, S // L, chunk, tuple(init for _ in hs))


def _mlstm(proj, gates, conv_w, conv_b, b_if, norm_g, batch, seq):
    G = MLSTM_HEADS
    HB = N_HEADS_D // G
    wqk = G * DQK_D
    wv = G * DV_D
    qk0 = 3 * W_C // wqk
    v0 = (3 * W_C + 2 * QK_D) // wv
    o0 = v0 + HB
    return pl.pallas_call(
        functools.partial(_mlstm_kernel, heads=G),
        out_shape=jax.ShapeDtypeStruct((batch * seq, W_D), BF16),
        grid=(batch, HB),
        in_specs=[pl.BlockSpec((seq, wqk), lambda b, h: (b, qk0 + h)),
                  pl.BlockSpec((seq, wqk), lambda b, h: (b, qk0 + HB + h)),
                  pl.BlockSpec((seq, wv), lambda b, h: (b, v0 + h)),
                  pl.BlockSpec((seq, wv), lambda b, h: (b, o0 + h)),
                  pl.BlockSpec((seq, GATE_PAD), lambda b, h: (b, 0)),
                  pl.BlockSpec((4, wqk), lambda b, h: (0, h)),
                  pl.BlockSpec((1, wqk), lambda b, h: (0, h)),
                  pl.BlockSpec((4, wqk), lambda b, h: (0, HB + h)),
                  pl.BlockSpec((1, wqk), lambda b, h: (0, HB + h)),
                  pl.BlockSpec((1, GATE_PAD), lambda b, h: (0, 0)),
                  pl.BlockSpec((1, wv), lambda b, h: (0, h))],
        out_specs=pl.BlockSpec((seq, wv), lambda b, h: (b, h)),
        scratch_shapes=[pltpu.VMEM((seq, wqk), BF16), pltpu.VMEM((seq, wqk), BF16)],
        compiler_params=_cparams(("parallel", "arbitrary")),
        name="mlstm",
    )(proj, proj, proj, proj, gates, conv_w, conv_b, conv_w, conv_b, b_if, norm_g)


def _conv_ffn_block(res, xb, w_up, w_gate, conv_w, conv_b, w_down_bf16, layer, ln_g, ln_b, seq, final):
    h = _ffn_up(xb, w_up, w_gate, conv_w, conv_b.reshape(DEPTH, 1, D_FF), layer, seq)
    z = _ffn_down_resid(h, w_down_bf16, res, layer)
    return _layer_norm(z, ln_g, ln_b, final=final)


def kernel(x, w_in_ab, w_out_ab, w_in_cd, b_if_cd, conv_cd, conv_cd_b, norm_cd_g, w_out_cd,
           ffn_w_up, ffn_w_gate, ffn_conv, ffn_conv_b, ffn_w_down, ln_g, ln_b):
    B, S, D = x.shape
    x = x.reshape(B * S, D)
    xb = x.astype(BF16)
    w_down = ffn_w_down.astype(BF16)
    ffn = (ffn_w_up, ffn_w_gate, ffn_conv, ffn_conv_b, w_down)

    qkv = _matmul(xb, w_in_ab, 0, IN_MAIN, BF16)
    ya = _attn_a(qkv, jnp.asarray(_dilated_bias_table()), B, S)
    yb = _attn_b(qkv, jnp.asarray(_moba_mean_matrix(S), BF16), B, S)
    z = _out_proj_resid(ya, yb, w_out_ab, (x,))
    xb, res = _layer_norm(z, ln_g[0, 0], ln_b[0, 0])
    xb, res = _conv_ffn_block(res, xb, *ffn, 0, ln_g[0, 1], ln_b[0, 1], S, False)

    w_in_t = jnp.swapaxes(w_in_cd, 1, 2)
    proj = _matmul(xb, w_in_t, 0, IN_MAIN, BF16, w_transposed=True)
    gates = _matmul(xb, w_in_t, IN_MAIN, GATE_PAD, F32, w_transposed=True)
    yc = _attn_c(proj, B, S)
    b_if = jnp.pad(b_if_cd[0], ((0, GATE_PAD - 2 * N_HEADS_D),)).reshape(1, GATE_PAD)
    yd = _mlstm(proj, gates, conv_cd[0], conv_cd_b[0].reshape(1, 2 * QK_D), b_if,
                norm_cd_g[0].reshape(1, W_D), B, S)
    z = _out_proj_resid(yc, yd, w_out_cd, res)
    xb, res = _layer_norm(z, ln_g[1, 0], ln_b[1, 0])
    out = _conv_ffn_block(res, xb, *ffn, 1, ln_g[1, 1], ln_b[1, 1], S, True)
    return out.reshape(B, S, D)
```
